```python
import jax, jax.numpy as jnp
from jax import lax
import numpy as np

D_MODEL = 4096
BATCH = 4
SEQ = 4096
DEPTH = 4
DEC_BATCH = 8
DEC_SEQ = 16
PAST_LEN = 2048

CHUNK = 64
HEAD_DIM = 128
N_HEADS_A = D_MODEL // (2 * HEAD_DIM)
D_ATT = N_HEADS_A * HEAD_DIM
D_CONV = D_MODEL // 2
D_BR = D_ATT
CONV_W = 3
N_BRANCH = 2
D_FF = ((8 * D_MODEL // 3 + 255) // 256) * 256
QBLK = 128
EPS = 1e-6
D_IN = 3 * D_ATT + 3 * D_CONV + N_BRANCH * D_MODEL

kernel_name = "stickbreak_shortconv_gated_parallel_trunk"


def rmsnorm(x, g):
    xf = x.astype(jnp.float32)
    xf = xf * lax.rsqrt(jnp.mean(xf * xf, axis=-1, keepdims=True) + EPS)
    return xf.astype(x.dtype) * g


def project(x, g_mix, w_in, g_q, g_k):
    b, s, _ = x.shape
    h = rmsnorm(x, g_mix)
    p = h @ w_in
    idx = np.cumsum([D_ATT, D_ATT, D_ATT, D_CONV, D_CONV, D_CONV])
    q, k, v, bg, cg, xc, gl = jnp.split(p, idx, axis=-1)
    q = rmsnorm(q.reshape(b, s, N_HEADS_A, HEAD_DIM), g_q)
    k = rmsnorm(k.reshape(b, s, N_HEADS_A, HEAD_DIM), g_k)
    v = v.reshape(b, s, N_HEADS_A, HEAD_DIM)
    gates = jax.nn.sigmoid(gl.astype(jnp.float32)).astype(x.dtype).reshape(b, s, N_BRANCH, D_MODEL)
    return q, k, v, bg, cg, xc, gates


def sb_block(q, k, v, q_pos, k_pos):
    z = jnp.einsum('bqhd,bkhd->bhqk', q.astype(jnp.float32), k.astype(jnp.float32)) * (HEAD_DIM ** -0.5)
    mask = k_pos[None, :] < q_pos[:, None]
    l_neg = jnp.where(mask, jax.nn.log_sigmoid(-z), 0.0)
    tail = lax.cumsum(l_neg, axis=3, reverse=True) - l_neg
    a = jnp.where(mask, jnp.exp(jax.nn.log_sigmoid(z) + tail), 0.0)
    return jnp.einsum('bhqk,bkhd->bqhd', a.astype(v.dtype), v)


def sba_prompt(q, k, v):
    b, s, h, d = q.shape
    nb = s // QBLK
    qb = q.reshape(b, nb, QBLK, h, d).transpose(1, 0, 2, 3, 4)
    pos = jnp.arange(s, dtype=jnp.int32)
    pb = pos.reshape(nb, QBLK)
    out = lax.map(lambda a: sb_block(a[0], k, v, a[1], pos), (qb, pb))
    return out.transpose(1, 0, 2, 3, 4).reshape(b, s, h * d)


def short_conv(bg, cg, xc, conv_prev, w_conv):
    s = xc.shape[1]
    u = cg * xc
    ext = jnp.concatenate([conv_prev, u], axis=1)
    y = ext[:, 0:s] * w_conv[0] + ext[:, 1:s + 1] * w_conv[1] + ext[:, 2:s + 2] * w_conv[2]
    return bg * y, ext[:, -(CONV_W - 1):]


def merge_out(x, o_a, o_b, gates, w_br, w_out):
    u_a = o_a @ w_br[0]
    u_b = o_b @ w_br[1]
    m = gates[:, :, 0] * u_a + gates[:, :, 1] * u_b
    return x + m @ w_out


def swiglu(x, g_ffn, w1, w3, w2):
    h = rmsnorm(x, g_ffn)
    return x + (jax.nn.silu(h @ w1) * (h @ w3)) @ w2


def setup_inputs(seed: int = 0) -> dict:
    key = jax.random.key(seed)
    ks = jax.random.split(key, 18)
    n = jax.random.normal
    f = jnp.float32
    return {
        "x_prompt": n(ks[0], (BATCH, SEQ, D_MODEL), f),
        "x_sample": n(ks[1], (DEC_BATCH, DEC_SEQ, D_MODEL), f),
        "cache_k": n(ks[2], (DEPTH, DEC_BATCH, PAST_LEN, N_HEADS_A, HEAD_DIM), f),
        "cache_v": n(ks[3], (DEPTH, DEC_BATCH, PAST_LEN, N_HEADS_A, HEAD_DIM), f),
        "state_conv": n(ks[4], (DEPTH, DEC_BATCH, CONV_W - 1, D_CONV), f),
        "g_mix": 1.0 + 0.01 * n(ks[5], (DEPTH, D_MODEL), f),
        "w_in": n(ks[6], (DEPTH, D_MODEL, D_IN), f) * D_MODEL ** -0.5,
        "g_q": 1.0 + 0.01 * n(ks[7], (DEPTH, HEAD_DIM), f),
        "g_k": 1.0 + 0.01 * n(ks[8], (DEPTH, HEAD_DIM), f),
        "w_conv": n(ks[9], (DEPTH, CONV_W, D_CONV), f) * CONV_W ** -0.5,
        "w_br": n(ks[10], (DEPTH, N_BRANCH, D_BR, D_MODEL), f) * D_BR ** -0.5,
        "w_out": n(ks[11], (DEPTH, D_MODEL, D_MODEL), f) * D_MODEL ** -0.5,
        "g_ffn": 1.0 + 0.01 * n(ks[12], (DEPTH, D_MODEL), f),
        "w_ffn_gate": n(ks[13], (DEPTH, D_MODEL, D_FF), f) * D_MODEL ** -0.5,
        "w_ffn_up": n(ks[14], (DEPTH, D_MODEL, D_FF), f) * D_MODEL ** -0.5,
        "w_ffn_down": n(ks[15], (DEPTH, D_FF, D_MODEL), f) * D_FF ** -0.5,
    }


def reference(x_prompt, x_sample, cache_k, cache_v, state_conv, g_mix, w_in, g_q, g_k,
              w_conv, w_br, w_out, g_ffn, w_ffn_gate, w_ffn_up, w_ffn_down):
    xp = x_prompt
    xs = x_sample
    dec_seq = xs.shape[1]
    past = cache_k.shape[2]
    q_pos_s = past + jnp.arange(dec_seq, dtype=jnp.int32)
    k_pos_s = jnp.arange(past + dec_seq, dtype=jnp.int32)
    kp_l, vp_l, cp_l, ks_l, vs_l, cs_l = [], [], [], [], [], []
    for l in range(DEPTH):
        q, k, v, bg, cg, xc, gates = project(xp, g_mix[l], w_in[l], g_q[l], g_k[l])
        o_a = sba_prompt(q, k, v)
        zero_prev = jnp.zeros((xp.shape[0], CONV_W - 1, D_CONV), xp.dtype)
        o_b, conv_new = short_conv(bg, cg, xc, zero_prev, w_conv[l])
        xp = merge_out(xp, o_a, o_b, gates, w_br[l], w_out[l])
        xp = swiglu(xp, g_ffn[l], w_ffn_gate[l], w_ffn_up[l], w_ffn_down[l])
        kp_l.append(k)
        vp_l.append(v)
        cp_l.append(conv_new)
        q, k, v, bg, cg, xc, gates = project(xs, g_mix[l], w_in[l], g_q[l], g_k[l])
        k_all = jnp.concatenate([cache_k[l], k], axis=1)
        v_all = jnp.concatenate([cache_v[l], v], axis=1)
        o_a = sb_block(q, k_all, v_all, q_pos_s, k_pos_s).reshape(xs.shape[0], dec_seq, D_ATT)
        o_b, conv_new = short_conv(bg, cg, xc, state_conv[l], w_conv[l])
        xs = merge_out(xs, o_a, o_b, gates, w_br[l], w_out[l])
        xs = swiglu(xs, g_ffn[l], w_ffn_gate[l], w_ffn_up[l], w_ffn_down[l])
        ks_l.append(k)
        vs_l.append(v)
        cs_l.append(conv_new)
    k_prompt = jnp.stack(kp_l)
    v_prompt = jnp.stack(vp_l)
    conv_prompt = jnp.stack(cp_l)
    k_sample = jnp.stack(ks_l)
    v_sample = jnp.stack(vs_l)
    conv_sample = jnp.stack(cs_l)
    return (xp, xs, k_prompt, v_prompt, conv_prompt, k_sample, v_sample, conv_sample)
```

```python
import functools
import math

import jax
import jax.numpy as jnp
from jax import lax
from jax.experimental import pallas as pl
from jax.experimental.pallas import tpu as pltpu

F32 = jnp.float32
BF16 = jnp.bfloat16

HEAD_DIM = 128
EPS = 1e-6
LANES = 128
SUBLANES = 8
VMEM_LIMIT_BYTES = 56 * 1024 * 1024
FF_ALIGN = 1024


def _params(n_grid):
    return pltpu.CompilerParams(dimension_semantics=("arbitrary",) * n_grid,
                                vmem_limit_bytes=VMEM_LIMIT_BYTES)


def _pick(dim, pref):
    if dim <= pref:
        return dim
    b = (pref // LANES) * LANES
    while b >= LANES:
        if dim % b == 0:
            return b
        b -= LANES
    return dim


def _rmsnorm_kernel(x_ref, g_ref, o_ref):
    x = x_ref[...]
    ms = jnp.mean(x * x, axis=-1, keepdims=True)
    o_ref[...] = ((x * lax.rsqrt(ms + EPS)) * g_ref[...]).astype(o_ref.dtype)


def _rmsnorm(x, g):
    m, d = x.shape
    tm = _pick(m, 256)
    return pl.pallas_call(
        _rmsnorm_kernel,
        grid=(m // tm,),
        in_specs=[pl.BlockSpec((tm, d), lambda i: (i, 0)),
                  pl.BlockSpec((1, d), lambda i: (0, 0))],
        out_specs=pl.BlockSpec((tm, d), lambda i: (i, 0)),
        out_shape=jax.ShapeDtypeStruct((m, d), BF16),
        compiler_params=_params(1),
        name="rmsnorm",
    )(x, g.reshape(1, d))


def _mm_kernel(*refs, n_a, n_b, pairs, n_extra, n_out, nk, epilogue):
    a_refs = refs[:n_a]
    b_refs = refs[n_a:n_a + n_b]
    extra = refs[n_a + n_b:n_a + n_b + n_extra]
    outs = refs[n_a + n_b + n_extra:n_a + n_b + n_extra + n_out]
    acc_refs = refs[n_a + n_b + n_extra + n_out:]

    def dots():
        return [jnp.dot(a_refs[ia][...], b_refs[ib][...], preferred_element_type=F32)
                for ia, ib in pairs]

    if nk == 1:
        epilogue(dots(), extra, outs)
        return

    k = pl.program_id(2)

    @pl.when(k == 0)
    def _():
        for r, d in zip(acc_refs, dots()):
            r[...] = d

    @pl.when(k > 0)
    def _():
        for r, d in zip(acc_refs, dots()):
            r[...] += d

    @pl.when(k == nk - 1)
    def _():
        epilogue([r[...] for r in acc_refs], extra, outs)


def _matmul(a_list, b_list, pairs, extras, outs, epilogue, *, n_cols, bm, bn, tk=None, name):
    m, kdim = a_list[0].shape
    bm = _pick(m, bm)
    offs = [off for _, off in b_list] + [off for _, _, off in extras]
    bn = _pick(math.gcd(n_cols, *offs), bn)
    tk = kdim if tk is None else tk
    nk = kdim // tk
    assert m % bm == 0 and n_cols % bn == 0 and kdim % tk == 0
    in_specs = [pl.BlockSpec((bm, tk), lambda j, i, k: (i, k)) for _ in a_list]
    args = list(a_list)
    for arr, off in b_list:
        assert off % bn == 0
        in_specs.append(pl.BlockSpec((tk, bn), lambda j, i, k, o=off // bn: (k, j + o)))
        args.append(arr)
    for arr, kind, off in extras:
        assert off % bn == 0
        if kind == "tile":
            in_specs.append(pl.BlockSpec((bm, bn), lambda j, i, k, o=off // bn: (i, j + o)))
        else:
            in_specs.append(pl.BlockSpec((1, bn), lambda j, i, k, o=off // bn: (0, j + o)))
        args.append(arr)
    out_specs = [pl.BlockSpec((bm, bn), lambda j, i, k: (i, j)) for _ in outs]
    out_shape = [jax.ShapeDtypeStruct((m, n_cols), dt) for dt in outs]
    scratch = [pltpu.VMEM((bm, bn), F32) for _ in pairs] if nk > 1 else []
    body = functools.partial(_mm_kernel, n_a=len(a_list), n_b=len(b_list), pairs=tuple(pairs),
                             n_extra=len(extras), n_out=len(outs), nk=nk, epilogue=epilogue)
    res = pl.pallas_call(
        body,
        grid=(n_cols // bn, m // bm, nk),
        in_specs=in_specs,
        out_specs=out_specs,
        out_shape=out_shape,
        scratch_shapes=scratch,
        compiler_params=_params(3),
        name=name,
    )(*args)
    return res


def _epi_store(accs, extra, outs):
    for o in outs:
        o[...] = accs[0].astype(o.dtype)


def _epi_headnorm(accs, extra, outs):
    x = accs[0]
    g = extra[0][...]
    for h in range(x.shape[1] // HEAD_DIM):
        sl = slice(h * HEAD_DIM, (h + 1) * HEAD_DIM)
        xh = x[:, sl]
        ms = jnp.mean(xh * xh, axis=-1, keepdims=True)
        y = (xh * lax.rsqrt(ms + EPS)) * g[:, sl]
        for o in outs:
            o[:, sl] = y.astype(o.dtype)


def _epi_merge(accs, extra, outs):
    ga = jax.nn.sigmoid(extra[0][...])
    gb = jax.nn.sigmoid(extra[1][...])
    outs[0][...] = (ga * accs[0] + gb * accs[1]).astype(outs[0].dtype)


def _epi_resid(accs, extra, outs):
    outs[0][...] = extra[0][...] + accs[0]


def _epi_swiglu(accs, extra, outs):
    g = accs[0]
    outs[0][...] = ((g * jax.nn.sigmoid(g)) * accs[1]).astype(outs[0].dtype)


def _conv_kernel(bg_ref, cg_ref, xc_ref, cgp_ref, xcp_ref, init_ref, w_ref, ob_ref, cn_ref, buf_ref, *, ts):
    s = pl.program_id(2)
    u = cg_ref[...] * xc_ref[...]
    prev_u = cgp_ref[SUBLANES - 2:, :] * xcp_ref[SUBLANES - 2:, :]
    prev = jnp.where(s == 0, init_ref[...], prev_u)
    buf_ref[SUBLANES - 2:SUBLANES, :] = prev
    buf_ref[SUBLANES:, :] = u
    u1 = buf_ref[pl.ds(SUBLANES - 1, ts), :]
    u2 = buf_ref[pl.ds(SUBLANES - 2, ts), :]
    w = w_ref[...]
    y = u2 * w[0:1, :] + u1 * w[1:2, :] + u * w[2:3, :]
    ob_ref[...] = (bg_ref[...] * y).astype(ob_ref.dtype)
    cn_ref[...] = buf_ref[pl.ds(ts + SUBLANES - 2, 2), :]


def _short_conv(bcx, init, w, batch, seq):
    c = w.shape[1]
    ts = _pick(seq, 512)
    tc = _pick(c, 512)
    ncb = c // tc
    x3 = bcx.reshape(batch, seq, 3 * c)
    sub = ts // SUBLANES

    def cur(region):
        return pl.BlockSpec((None, ts, tc), lambda b, j, s, r=region: (b, s, j + r * ncb))

    def prev(region):
        return pl.BlockSpec((None, SUBLANES, tc),
                            lambda b, j, s, r=region: (b, jnp.maximum(s * sub - 1, 0), j + r * ncb))

    ob, cn = pl.pallas_call(
        functools.partial(_conv_kernel, ts=ts),
        grid=(batch, ncb, seq // ts),
        in_specs=[cur(0), cur(1), cur(2), prev(1), prev(2),
                  pl.BlockSpec((None, 2, tc), lambda b, j, s: (b, 0, j)),
                  pl.BlockSpec((3, tc), lambda b, j, s: (0, j))],
        out_specs=[pl.BlockSpec((None, ts, tc), lambda b, j, s: (b, s, j)),
                   pl.BlockSpec((None, 2, tc), lambda b, j, s: (b, 0, j))],
        out_shape=[jax.ShapeDtypeStruct((batch, seq, c), BF16),
                   jax.ShapeDtypeStruct((batch, 2, c), F32)],
        scratch_shapes=[pltpu.VMEM((ts + SUBLANES, tc), F32)],
        compiler_params=_params(3),
        name="short_conv",
    )(x3, x3, x3, x3, x3, init, w)
    return ob.reshape(batch * seq, c), cn


def _sb_block(q, kb, vb, u_tri, carry, acc, mask, scale):
    z = lax.dot_general(q, kb, (((1,), (1,)), ((), ())), preferred_element_type=F32) * scale
    sp = jnp.maximum(z, 0.0) + jnp.log(1.0 + jnp.exp(-jnp.abs(z)))
    lneg = -sp
    if mask is not None:
        lneg = jnp.where(mask, lneg, 0.0)
    hi = lneg.astype(BF16)
    lo = (lneg - hi.astype(F32)).astype(BF16)
    tail = (jnp.dot(hi, u_tri, preferred_element_type=F32)
            + jnp.dot(lo, u_tri, preferred_element_type=F32))
    a = jnp.exp((z - sp) + tail + carry)
    if mask is not None:
        a = jnp.where(mask, a, 0.0)
    acc = acc + jnp.dot(a.astype(BF16), vb, preferred_element_type=F32)
    carry = carry + jnp.sum(lneg, axis=1, keepdims=True)
    return carry, acc


def _sba_prompt_kernel(q_ref, k_ref, v_ref, u_ref, o_ref, *, t, scale):
    i = pl.program_id(2)
    q = q_ref[...]
    u_tri = u_ref[...]
    row = lax.broadcasted_iota(jnp.int32, (t, t), 0)
    col = lax.broadcasted_iota(jnp.int32, (t, t), 1)
    start = pl.multiple_of(i * t, t)
    carry = jnp.zeros((t, 1), F32)
    acc = jnp.zeros((t, HEAD_DIM), F32)
    carry, acc = _sb_block(q, k_ref[pl.ds(start, t), :], v_ref[pl.ds(start, t), :], u_tri,
                           carry, acc, col < row, scale)

    def body(n, c):
        st = pl.multiple_of((i - 1 - n) * t, t)
        return _sb_block(q, k_ref[pl.ds(st, t), :], v_ref[pl.ds(st, t), :], u_tri, c[0], c[1], None, scale)

    carry, acc = lax.fori_loop(0, i, body, (carry, acc))
    o_ref[...] = acc.astype(o_ref.dtype)


def _tri(t):
    r = lax.broadcasted_iota(jnp.int32, (t, t), 0)
    c = lax.broadcasted_iota(jnp.int32, (t, t), 1)
    return (r > c).astype(BF16)


def _sba_prompt(q, k, v, batch, seq):
    d_att = q.shape[1]
    heads = d_att // HEAD_DIM
    t = _pick(seq, 256)
    q3, k3, v3 = (a.reshape(batch, seq, d_att) for a in (q, k, v))
    out = pl.pallas_call(
        functools.partial(_sba_prompt_kernel, t=t, scale=HEAD_DIM ** -0.5),
        grid=(batch, heads, seq // t),
        in_specs=[pl.BlockSpec((None, t, HEAD_DIM), lambda b, h, i: (b, i, h)),
                  pl.BlockSpec((None, seq, HEAD_DIM), lambda b, h, i: (b, 0, h)),
                  pl.BlockSpec((None, seq, HEAD_DIM), lambda b, h, i: (b, 0, h)),
                  pl.BlockSpec((t, t), lambda b, h, i: (0, 0))],
        out_specs=pl.BlockSpec((None, t, HEAD_DIM), lambda b, h, i: (b, i, h)),
        out_shape=jax.ShapeDtypeStruct((batch, seq, d_att), BF16),
        compiler_params=_params(3),
        name="sba_prompt",
    )(q3, k3, v3, _tri(t))
    return out.reshape(batch * seq, d_att)


def _sba_sample_kernel(q_ref, kn_ref, vn_ref, kc_ref, vc_ref, u_ref, o_ref, *, nq, tk, past, scale):
    q = q_ref[...]
    u_tri = u_ref[...]
    row = lax.broadcasted_iota(jnp.int32, (nq, tk), 0)
    col = lax.broadcasted_iota(jnp.int32, (nq, tk), 1)
    carry = jnp.zeros((nq, 1), F32)
    acc = jnp.zeros((nq, HEAD_DIM), F32)
    carry, acc = _sb_block(q, kn_ref[...], vn_ref[...], u_tri, carry, acc, col < row, scale)
    nblk = past // tk

    def body(n, c):
        st = pl.multiple_of((nblk - 1 - n) * tk, tk)
        kb = kc_ref[pl.ds(st, tk), :].astype(BF16)
        vb = vc_ref[pl.ds(st, tk), :].astype(BF16)
        return _sb_block(q, kb, vb, u_tri, c[0], c[1], None, scale)

    carry, acc = lax.fori_loop(0, nblk, body, (carry, acc))
    o_ref[...] = acc.astype(o_ref.dtype)


def _sba_sample(q, k_new, v_new, cache_k, cache_v, batch, nq):
    d_att = q.shape[1]
    heads = d_att // HEAD_DIM
    past = cache_k.shape[1]
    tk = _pick(past, 128)
    assert nq <= tk and past % tk == 0
    q3 = q.reshape(batch, nq, d_att)
    pad = lambda a: jnp.pad(a.reshape(batch, nq, d_att), ((0, 0), (0, tk - nq), (0, 0)))
    kc = cache_k.reshape(batch, past, d_att)
    vc = cache_v.reshape(batch, past, d_att)
    out = pl.pallas_call(
        functools.partial(_sba_sample_kernel, nq=nq, tk=tk, past=past, scale=HEAD_DIM ** -0.5),
        grid=(batch, heads),
        in_specs=[pl.BlockSpec((None, nq, HEAD_DIM), lambda b, h: (b, 0, h)),
                  pl.BlockSpec((None, tk, HEAD_DIM), lambda b, h: (b, 0, h)),
                  pl.BlockSpec((None, tk, HEAD_DIM), lambda b, h: (b, 0, h)),
                  pl.BlockSpec((None, past, HEAD_DIM), lambda b, h: (b, 0, h)),
                  pl.BlockSpec((None, past, HEAD_DIM), lambda b, h: (b, 0, h)),
                  pl.BlockSpec((tk, tk), lambda b, h: (0, 0))],
        out_specs=pl.BlockSpec((None, nq, HEAD_DIM), lambda b, h: (b, 0, h)),
        out_shape=jax.ShapeDtypeStruct((batch, nq, d_att), BF16),
        compiler_params=_params(2),
        name="sba_sample",
    )(q3, pad(k_new), pad(v_new), kc, vc, _tri(tk))
    return out.reshape(batch * nq, d_att)


def _layer(x, batch, seq, conv_init, cache, w, bm):
    d_model = x.shape[1]
    d_att = w["g_qt"].shape[1]
    d_conv = w["w_conv"].shape[1]
    w_in = w["w_in"]
    bn = 1024
    h = _rmsnorm(x, w["g_mix"])
    (q,) = _matmul([h], [(w_in, 0)], [(0, 0)], [(w["g_qt"], "row", 0)], [BF16], _epi_headnorm,
                   n_cols=d_att, bm=bm, bn=bn, name="proj_q")
    k, k16 = _matmul([h], [(w_in, d_att)], [(0, 0)], [(w["g_kt"], "row", 0)], [F32, BF16], _epi_headnorm,
                     n_cols=d_att, bm=bm, bn=bn, name="proj_k")
    v, v16 = _matmul([h], [(w_in, 2 * d_att)], [(0, 0)], [], [F32, BF16], _epi_store,
                     n_cols=d_att, bm=bm, bn=bn, name="proj_v")
    (bcx,) = _matmul([h], [(w_in, 3 * d_att)], [(0, 0)], [], [F32], _epi_store,
                     n_cols=3 * d_conv, bm=bm, bn=bn, name="proj_conv")
    (gl,) = _matmul([h], [(w_in, 3 * d_att + 3 * d_conv)], [(0, 0)], [], [F32], _epi_store,
                    n_cols=2 * d_model, bm=bm, bn=bn, name="proj_gate")
    if cache is None:
        o_a = _sba_prompt(q, k16, v16, batch, seq)
    else:
        o_a = _sba_sample(q, k16, v16, cache[0], cache[1], batch, seq)
    o_b, conv_new = _short_conv(bcx, conv_init, w["w_conv"], batch, seq)
    (m,) = _matmul([o_a, o_b], [(w["w_bra"], 0), (w["w_brb"], 0)], [(0, 0), (1, 1)],
                   [(gl, "tile", 0), (gl, "tile", d_model)], [BF16], _epi_merge,
                   n_cols=d_model, bm=bm, bn=512, name="merge")
    (x,) = _matmul([m], [(w["w_out"], 0)], [(0, 0)], [(x, "tile", 0)], [F32], _epi_resid,
                   n_cols=d_model, bm=bm, bn=bn, name="out_proj")
    h2 = _rmsnorm(x, w["g_ffn"])
    d_ffp = w["w1"].shape[1]
    (hid,) = _matmul([h2], [(w["w1"], 0), (w["w3"], 0)], [(0, 0), (0, 1)], [], [BF16], _epi_swiglu,
                     n_cols=d_ffp, bm=bm, bn=512, name="ffn_up")
    (x,) = _matmul([hid], [(w["w2"], 0)], [(0, 0)], [(x, "tile", 0)], [F32], _epi_resid,
                   n_cols=d_model, bm=bm, bn=bn, tk=_pick(d_ffp, 2816), name="ffn_down")
    return x, k, v, conv_new


def kernel(x_prompt, x_sample, cache_k, cache_v, state_conv, g_mix, w_in, g_q, g_k, w_conv, w_br, w_out,
           g_ffn, w_ffn_gate, w_ffn_up, w_ffn_down):
    batch, seq, d_model = x_prompt.shape
    dec_batch, dec_seq, _ = x_sample.shape
    depth = w_in.shape[0]
    d_conv = w_conv.shape[2]
    heads = cache_k.shape[3]
    d_att = heads * HEAD_DIM
    d_ff = w_ffn_gate.shape[2]
    ff_pad = (-d_ff) % FF_ALIGN

    xp = x_prompt.reshape(batch * seq, d_model)
    xs = x_sample.reshape(dec_batch * dec_seq, d_model)
    zero_state = jnp.zeros((batch, 2, d_conv), F32)
    kp, vp, cp, ks, vs, cs = [], [], [], [], [], []
    for l in range(depth):
        w = {
            "g_mix": g_mix[l], "g_ffn": g_ffn[l],
            "g_qt": jnp.tile(g_q[l], heads).reshape(1, d_att),
            "g_kt": jnp.tile(g_k[l], heads).reshape(1, d_att),
            "w_in": w_in[l].astype(BF16),
            "w_conv": w_conv[l],
            "w_bra": w_br[l, 0].astype(BF16), "w_brb": w_br[l, 1].astype(BF16),
            "w_out": w_out[l].astype(BF16),
            "w1": jnp.pad(w_ffn_gate[l].astype(BF16), ((0, 0), (0, ff_pad))),
            "w3": jnp.pad(w_ffn_up[l].astype(BF16), ((0, 0), (0, ff_pad))),
            "w2": jnp.pad(w_ffn_down[l].astype(BF16), ((0, ff_pad), (0, 0))),
        }
        xp, k, v, cn = _layer(xp, batch, seq, zero_state, None, w, bm=1024)
        kp.append(k); vp.append(v); cp.append(cn)
        xs, k, v, cn = _layer(xs, dec_batch, dec_seq, state_conv[l], (cache_k[l], cache_v[l]), w, bm=128)
        ks.append(k); vs.append(v); cs.append(cn)

    kv_p = lambda lst: jnp.stack(lst).reshape(depth, batch, seq, heads, HEAD_DIM)
    kv_s = lambda lst: jnp.stack(lst).reshape(depth, dec_batch, dec_seq, heads, HEAD_DIM)
    return (xp.reshape(batch, seq, d_model), xs.reshape(dec_batch, dec_seq, d_model),
            kv_p(kp), kv_p(vp), jnp.stack(cp), kv_s(ks), kv_s(vs), jnp.stack(cs))
```

```python
import functools
import math

import jax
import jax.numpy as jnp
from jax import lax
from jax.experimental import pallas as pl
from jax.experimental.pallas import tpu as pltpu

F32 = jnp.float32
BF16 = jnp.bfloat16

HEAD_DIM = 128
EPS = 1e-6
LANES = 128
SUBLANES = 8
VMEM_LIMIT_BYTES = 56 * 1024 * 1024
LOG_WEIGHT_FLOOR = -104.0
FF_TK = 2816


def _params(n_grid):
    return pltpu.CompilerParams(dimension_semantics=("arbitrary",) * n_grid,
                                vmem_limit_bytes=VMEM_LIMIT_BYTES)


def _pick(dim, pref):
    if dim <= pref:
        return dim
    b = (pref // LANES) * LANES
    while b >= LANES:
        if dim % b == 0:
            return b
        b -= LANES
    return dim


def _head(h):
    return slice(h * HEAD_DIM, (h + 1) * HEAD_DIM)


def _rmsnorm_kernel(x_ref, g_ref, o_ref):
    x = x_ref[...]
    ms = jnp.mean(x * x, axis=-1, keepdims=True)
    o_ref[...] = ((x * lax.rsqrt(ms + EPS)) * g_ref[...]).astype(o_ref.dtype)


def _rmsnorm(x, g):
    m, d = x.shape
    tm = _pick(m, 256)
    return pl.pallas_call(
        _rmsnorm_kernel,
        grid=(m // tm,),
        in_specs=[pl.BlockSpec((tm, d), lambda i: (i, 0)),
                  pl.BlockSpec((1, d), lambda i: (0, 0))],
        out_specs=pl.BlockSpec((tm, d), lambda i: (i, 0)),
        out_shape=jax.ShapeDtypeStruct((m, d), BF16),
        compiler_params=_params(1),
        name="rmsnorm",
    )(x, g.reshape(1, d))


def _mm_kernel(*refs, n_a, n_b, pairs, n_extra, n_out, nk, k_rem, epilogue):
    a_refs = refs[:n_a]
    b_refs = refs[n_a:n_a + n_b]
    extra = refs[n_a + n_b:n_a + n_b + n_extra]
    outs = refs[n_a + n_b + n_extra:n_a + n_b + n_extra + n_out]
    acc_refs = refs[n_a + n_b + n_extra + n_out:n_a + n_b + n_extra + n_out + (len(pairs) if nk > 1 else 0)]

    def dots(kk=None):
        if kk is None:
            return [jnp.dot(a_refs[ia][...], b_refs[ib][...], preferred_element_type=F32) for ia, ib in pairs]
        return [jnp.dot(a_refs[ia][:, :kk], b_refs[ib][:kk, :], preferred_element_type=F32) for ia, ib in pairs]

    if nk == 1:
        epilogue(dots(), extra, outs)
        return

    k = pl.program_id(2)

    @pl.when(k == 0)
    def _():
        for r, d in zip(acc_refs, dots()):
            r[...] = d

    if nk > 2:
        @pl.when(jnp.logical_and(k > 0, k < nk - 1))
        def _():
            for r, d in zip(acc_refs, dots()):
                r[...] += d

    @pl.when(k == nk - 1)
    def _():
        epilogue([r[...] + d for r, d in zip(acc_refs, dots(k_rem))], extra, outs)


def _matmul(a_list, b_list, pairs, extras, outs, epilogue, *, n_cols, bm, bn, tk=None, name):
    m, kdim = a_list[0].shape
    bm = _pick(m, bm)
    offs = [off for _, _, off in b_list] + [off for _, _, off in extras]
    if extras or any(offs):
        bn = _pick(math.gcd(n_cols, *offs), bn)
    tk = kdim if tk is None else tk
    nk = pl.cdiv(kdim, tk)
    k_rem = kdim - (nk - 1) * tk
    assert m % bm == 0 and k_rem % LANES == 0 and (nk == 1) == (tk == kdim)
    nj = pl.cdiv(n_cols, bn)
    nmb = m // bm

    in_specs = [pl.BlockSpec((bm, tk), lambda j, i, k: (i, k)) for _ in a_list]
    args = list(a_list)
    for arr, layer, off in b_list:
        assert off % bn == 0
        in_specs.append(pl.BlockSpec((None, tk, bn), lambda j, i, k, o=off // bn, l=layer: (l, k, j + o)))
        args.append(arr)
    for arr, kind, off in extras:
        assert off % bn == 0
        if kind == "tile":
            in_specs.append(pl.BlockSpec((bm, bn), lambda j, i, k, o=off // bn: (i, j + o)))
        else:
            in_specs.append(pl.BlockSpec((1, bn), lambda j, i, k, o=off // bn: (0, j + o)))
        args.append(arr)
    out_specs, out_shape, aliases = [], [], {}
    for oi, o in enumerate(outs):
        if isinstance(o, tuple):
            _, dt, buf, depth, layer = o
            assert bn == SUBLANES * HEAD_DIM and n_cols % bn == 0
            out_specs.append(pl.BlockSpec((bm, None, SUBLANES, HEAD_DIM),
                                          lambda j, i, k, r=layer * nmb: (i + r, j, 0, 0)))
            out_shape.append(jax.ShapeDtypeStruct((depth * m, nj, SUBLANES, HEAD_DIM), dt))
            if buf is not None:
                in_specs.append(pl.BlockSpec(memory_space=pl.ANY))
                aliases[len(args)] = oi
                args.append(buf)
        else:
            out_specs.append(pl.BlockSpec((bm, bn), lambda j, i, k: (i, j)))
            out_shape.append(jax.ShapeDtypeStruct((m, n_cols), o))
    n_alias = len(aliases)
    scratch = [pltpu.VMEM((bm, bn), F32) for _ in pairs] if nk > 1 else []

    def body(*refs):
        n_in = len(a_list) + len(b_list) + len(extras)
        refs = refs[:n_in] + refs[n_in + n_alias:]
        _mm_kernel(*refs, n_a=len(a_list), n_b=len(b_list), pairs=tuple(pairs), n_extra=len(extras),
                   n_out=len(outs), nk=nk, k_rem=k_rem, epilogue=epilogue)

    return pl.pallas_call(
        body,
        grid=(nj, nmb, nk),
        in_specs=in_specs,
        out_specs=out_specs,
        out_shape=out_shape,
        scratch_shapes=scratch,
        input_output_aliases=aliases,
        compiler_params=_params(3),
        name=name,
    )(*args)


def _store_head(o_ref, h, y):
    if len(o_ref.shape) == 3:
        rows = o_ref.shape[0]
        o_ref.reshape(rows * SUBLANES, HEAD_DIM)[pl.ds(h, rows, stride=SUBLANES), :] = y.astype(o_ref.dtype)
    else:
        o_ref[:, _head(h)] = y.astype(o_ref.dtype)


def _epi_store(accs, extra, outs):
    for o in outs:
        if len(o.shape) == 3:
            for h in range(accs[0].shape[1] // HEAD_DIM):
                _store_head(o, h, accs[0][:, _head(h)])
        else:
            o[...] = accs[0].astype(o.dtype)


def _epi_headnorm(accs, extra, outs):
    x = accs[0]
    g = extra[0][...]
    for h in range(x.shape[1] // HEAD_DIM):
        xh = x[:, _head(h)]
        ms = jnp.mean(xh * xh, axis=-1, keepdims=True)
        y = (xh * lax.rsqrt(ms + EPS)) * g[:, _head(h)]
        for o in outs:
            _store_head(o, h, y)


def _epi_merge(accs, extra, outs):
    ga = jax.nn.sigmoid(extra[0][...])
    gb = jax.nn.sigmoid(extra[1][...])
    outs[0][...] = (ga * accs[0] + gb * accs[1]).astype(outs[0].dtype)


def _epi_resid(accs, extra, outs):
    outs[0][...] = extra[0][...] + accs[0]


def _epi_swiglu(accs, extra, outs):
    g = accs[0]
    outs[0][...] = ((g * jax.nn.sigmoid(g)) * accs[1]).astype(outs[0].dtype)


def _conv_kernel(bg_ref, cg_ref, xc_ref, cgp_ref, xcp_ref, init_ref, w_ref, ob_ref, cn_ref, buf_ref, *, ts):
    s = pl.program_id(2)
    u = cg_ref[...] * xc_ref[...]
    prev_u = cgp_ref[SUBLANES - 2:, :] * xcp_ref[SUBLANES - 2:, :]
    prev = jnp.where(s == 0, init_ref[...], prev_u)
    buf_ref[SUBLANES - 2:SUBLANES, :] = prev
    buf_ref[SUBLANES:, :] = u
    u1 = buf_ref[pl.ds(SUBLANES - 1, ts), :]
    u2 = buf_ref[pl.ds(SUBLANES - 2, ts), :]
    w = w_ref[...]
    y = u2 * w[0:1, :] + u1 * w[1:2, :] + u * w[2:3, :]
    ob_ref[...] = (bg_ref[...] * y).astype(ob_ref.dtype)
    cn_ref[...] = buf_ref[pl.ds(ts + SUBLANES - 2, 2), :]


def _short_conv(bcx, init, w, batch, seq):
    c = w.shape[1]
    ts = _pick(seq, 512)
    tc = _pick(c, 512)
    ncb = c // tc
    x3 = bcx.reshape(batch, seq, 3 * c)
    sub = ts // SUBLANES

    def cur(region):
        return pl.BlockSpec((None, ts, tc), lambda b, j, s, r=region: (b, s, j + r * ncb))

    def prev(region):
        return pl.BlockSpec((None, SUBLANES, tc),
                            lambda b, j, s, r=region: (b, jnp.maximum(s * sub - 1, 0), j + r * ncb))

    ob, cn = pl.pallas_call(
        functools.partial(_conv_kernel, ts=ts),
        grid=(batch, ncb, seq // ts),
        in_specs=[cur(0), cur(1), cur(2), prev(1), prev(2),
                  pl.BlockSpec((None, 2, tc), lambda b, j, s: (b, 0, j)),
                  pl.BlockSpec((3, tc), lambda b, j, s: (0, j))],
        out_specs=[pl.BlockSpec((None, ts, tc), lambda b, j, s: (b, s, j)),
                   pl.BlockSpec((None, 2, tc), lambda b, j, s: (b, 0, j))],
        out_shape=[jax.ShapeDtypeStruct((batch, seq, c), BF16),
                   jax.ShapeDtypeStruct((batch, 2, c), F32)],
        scratch_shapes=[pltpu.VMEM((ts + SUBLANES, tc), F32)],
        compiler_params=_params(3),
        name="short_conv",
    )(x3, x3, x3, x3, x3, init, w)
    return ob.reshape(batch * seq, c), cn


def _sb_block(q, kb, vb, u_tri, carry, acc, mask, scale):
    z = lax.dot_general(q, kb, (((1,), (1,)), ((), ())), preferred_element_type=F32) * scale
    sp = jnp.maximum(z, 0.0) + jnp.log(1.0 + jnp.exp(-jnp.abs(z)))
    lneg = -sp
    if mask is not None:
        lneg = jnp.where(mask, lneg, 0.0)
    hi = lneg.astype(BF16)
    lo = (lneg - hi.astype(F32)).astype(BF16)
    tail = (jnp.dot(hi, u_tri, preferred_element_type=F32)
            + jnp.dot(lo, u_tri, preferred_element_type=F32))
    a = jnp.exp((z - sp) + tail + carry)
    if mask is not None:
        a = jnp.where(mask, a, 0.0)
    acc = acc + jnp.dot(a.astype(BF16), vb, preferred_element_type=F32)
    carry = carry + jnp.sum(lneg, axis=1, keepdims=True)
    return carry, acc


def _any_weight_left(carries):
    m = carries[0]
    for c in carries[1:]:
        m = jnp.maximum(m, c)
    return (jnp.max(m) > LOG_WEIGHT_FLOOR).astype(jnp.int32)


def _sba_prompt_kernel(q_ref, k_ref, v_ref, u_ref, o_ref, *, t, hp, scale):
    i = pl.program_id(2)
    u_tri = u_ref[...]
    row = lax.broadcasted_iota(jnp.int32, (t, t), 0)
    col = lax.broadcasted_iota(jnp.int32, (t, t), 1)
    qs = [q_ref[:, _head(h)] for h in range(hp)]

    def blocks(j, carries, accs, mask):
        st = pl.multiple_of(j * t, t)
        res = [_sb_block(qs[h], k_ref[pl.ds(st, t), _head(h)], v_ref[pl.ds(st, t), _head(h)], u_tri,
                         carries[h], accs[h], mask, scale) for h in range(hp)]
        return tuple(r[0] for r in res), tuple(r[1] for r in res)

    zero_c = tuple(jnp.zeros((t, 1), F32) for _ in range(hp))
    zero_a = tuple(jnp.zeros((t, HEAD_DIM), F32) for _ in range(hp))
    carries, accs = blocks(i, zero_c, zero_a, col < row)

    def cond(c):
        return jnp.logical_and(c[0] >= 0, c[1] > 0)

    def body(c):
        j, _, carries, accs = c
        carries, accs = blocks(j, carries, accs, None)
        return j - 1, _any_weight_left(carries), carries, accs

    _, _, carries, accs = lax.while_loop(cond, body, (i - 1, _any_weight_left(carries), carries, accs))
    for h in range(hp):
        o_ref[:, _head(h)] = accs[h].astype(o_ref.dtype)


def _tri(t):
    r = lax.broadcasted_iota(jnp.int32, (t, t), 0)
    c = lax.broadcasted_iota(jnp.int32, (t, t), 1)
    return (r > c).astype(BF16)


def _sba_prompt(q, k, v, batch, seq):
    d_att = q.shape[1]
    heads = d_att // HEAD_DIM
    hp = 2 if heads % 2 == 0 else 1
    t = _pick(seq, 256)
    q3, k3, v3 = (a.reshape(batch, seq, d_att) for a in (q, k, v))
    out = pl.pallas_call(
        functools.partial(_sba_prompt_kernel, t=t, hp=hp, scale=HEAD_DIM ** -0.5),
        grid=(batch, heads // hp, seq // t),
        in_specs=[pl.BlockSpec((None, t, hp * HEAD_DIM), lambda b, g, i: (b, i, g)),
                  pl.BlockSpec((None, seq, hp * HEAD_DIM), lambda b, g, i: (b, 0, g)),
                  pl.BlockSpec((None, seq, hp * HEAD_DIM), lambda b, g, i: (b, 0, g)),
                  pl.BlockSpec((t, t), lambda b, g, i: (0, 0))],
        out_specs=pl.BlockSpec((None, t, hp * HEAD_DIM), lambda b, g, i: (b, i, g)),
        out_shape=jax.ShapeDtypeStruct((batch, seq, d_att), BF16),
        compiler_params=_params(3),
        name="sba_prompt",
    )(q3, k3, v3, _tri(t))
    return out.reshape(batch * seq, d_att)


def _sba_sample_kernel(q_ref, kn_ref, vn_ref, kc_ref, vc_ref, u_ref, o_ref, carry_ref, acc_ref, alive_ref,
                       *, nq, tk, heads, scale):
    s = pl.program_id(1)
    u_tri = u_ref[...]

    def set_alive():
        alive_ref[0] = _any_weight_left([carry_ref[h] for h in range(heads)])

    @pl.when(s == 0)
    def _():
        row = lax.broadcasted_iota(jnp.int32, (nq, tk), 0)
        col = lax.broadcasted_iota(jnp.int32, (nq, tk), 1)
        for h in range(heads):
            carry, acc = _sb_block(q_ref[:, _head(h)], kn_ref[:, _head(h)], vn_ref[:, _head(h)], u_tri,
                                   jnp.zeros((nq, 1), F32), jnp.zeros((nq, HEAD_DIM), F32), col < row, scale)
            carry_ref[h] = carry
            acc_ref[h] = acc
        set_alive()

    @pl.when(alive_ref[0] > 0)
    def _():
        for h in range(heads):
            kb = kc_ref[pl.ds(h, tk, stride=heads), :].astype(BF16)
            vb = vc_ref[pl.ds(h, tk, stride=heads), :].astype(BF16)
            carry, acc = _sb_block(q_ref[:, _head(h)], kb, vb, u_tri, carry_ref[h], acc_ref[h], None, scale)
            carry_ref[h] = carry
            acc_ref[h] = acc
        set_alive()

    for h in range(heads):
        o_ref[:, _head(h)] = acc_ref[h].astype(o_ref.dtype)


def _sba_sample(q, k_new, v_new, cache_k, cache_v, layer, batch, nq):
    d_att = q.shape[1]
    heads = d_att // HEAD_DIM
    depth, _, past = cache_k.shape[:3]
    tk = _pick(past, 128)
    assert nq <= tk and past % tk == 0
    nkb = past // tk
    q3 = q.reshape(batch, nq, d_att)
    pad = lambda a: jnp.pad(a.reshape(batch, nq, d_att), ((0, 0), (0, tk - nq), (0, 0)))
    kc = cache_k.reshape(depth, batch, past * heads, HEAD_DIM)
    vc = cache_v.reshape(depth, batch, past * heads, HEAD_DIM)
    cache_spec = pl.BlockSpec((None, None, tk * heads, HEAD_DIM), lambda b, s: (layer, b, nkb - 1 - s, 0))
    out = pl.pallas_call(
        functools.partial(_sba_sample_kernel, nq=nq, tk=tk, heads=heads, scale=HEAD_DIM ** -0.5),
        grid=(batch, nkb),
        in_specs=[pl.BlockSpec((None, nq, d_att), lambda b, s: (b, 0, 0)),
                  pl.BlockSpec((None, tk, d_att), lambda b, s: (b, 0, 0)),
                  pl.BlockSpec((None, tk, d_att), lambda b, s: (b, 0, 0)),
                  cache_spec, cache_spec,
                  pl.BlockSpec((tk, tk), lambda b, s: (0, 0))],
        out_specs=pl.BlockSpec((None, nq, d_att), lambda b, s: (b, 0, 0)),
        out_shape=jax.ShapeDtypeStruct((batch, nq, d_att), BF16),
        scratch_shapes=[pltpu.VMEM((heads, nq, 1), F32), pltpu.VMEM((heads, nq, HEAD_DIM), F32),
                        pltpu.SMEM((1,), jnp.int32)],
        compiler_params=_params(2),
        name="sba_sample",
    )(q3, pad(k_new), pad(v_new), kc, vc, _tri(tk))
    return out.reshape(batch * nq, d_att)


def _layer(x, batch, seq, conv_init, cache, w, layer, depth, kv_bufs, bm):
    d_model = x.shape[1]
    d_att = w["g_qt"].shape[1]
    d_conv = w["w_conv"].shape[1]
    w_in = w["w_in"]
    bn = SUBLANES * HEAD_DIM
    h = _rmsnorm(x, w["g_mix"][layer])
    (q,) = _matmul([h], [(w_in, layer, 0)], [(0, 0)], [(w["g_qt"], "row", 0)], [BF16], _epi_headnorm,
                   n_cols=d_att, bm=bm, bn=bn, name="proj_q")
    k_buf, k16 = _matmul([h], [(w_in, layer, d_att)], [(0, 0)], [(w["g_kt"], "row", 0)],
                         [("heads", F32, kv_bufs[0], depth, layer), BF16], _epi_headnorm,
                         n_cols=d_att, bm=bm, bn=bn, name="proj_k")
    v_buf, v16 = _matmul([h], [(w_in, layer, 2 * d_att)], [(0, 0)], [],
                         [("heads", F32, kv_bufs[1], depth, layer), BF16], _epi_store,
                         n_cols=d_att, bm=bm, bn=bn, name="proj_v")
    (bcx,) = _matmul([h], [(w_in, layer, 3 * d_att)], [(0, 0)], [], [F32], _epi_store,
                     n_cols=3 * d_conv, bm=bm, bn=bn, name="proj_conv")
    (gl,) = _matmul([h], [(w_in, layer, 3 * d_att + 3 * d_conv)], [(0, 0)], [], [F32], _epi_store,
                    n_cols=2 * d_model, bm=bm, bn=bn, name="proj_gate")
    if cache is None:
        o_a = _sba_prompt(q, k16, v16, batch, seq)
    else:
        o_a = _sba_sample(q, k16, v16, cache[0], cache[1], layer, batch, seq)
    o_b, conv_new = _short_conv(bcx, conv_init, w["w_conv"], batch, seq)
    (m,) = _matmul([o_a, o_b], [(w["w_br"], 2 * layer, 0), (w["w_br"], 2 * layer + 1, 0)], [(0, 0), (1, 1)],
                   [(gl, "tile", 0), (gl, "tile", d_model)], [BF16], _epi_merge,
                   n_cols=d_model, bm=bm, bn=512, name="merge")
    (x,) = _matmul([m], [(w["w_out"], layer, 0)], [(0, 0)], [(x, "tile", 0)], [F32], _epi_resid,
                   n_cols=d_model, bm=bm, bn=bn, name="out_proj")
    h2 = _rmsnorm(x, w["g_ffn"][layer])
    d_ff = w["w1"].shape[2]
    (hid,) = _matmul([h2], [(w["w1"], layer, 0), (w["w3"], layer, 0)], [(0, 0), (0, 1)], [], [BF16], _epi_swiglu,
                     n_cols=d_ff, bm=bm, bn=512, name="ffn_up")
    (x,) = _matmul([hid], [(w["w2"], layer, 0)], [(0, 0)], [(x, "tile", 0)], [F32], _epi_resid,
                   n_cols=d_model, bm=bm, bn=bn, tk=w["ff_tk"], name="ffn_down")
    return x, k_buf, v_buf, conv_new


def kernel(x_prompt, x_sample, cache_k, cache_v, state_conv, g_mix, w_in, g_q, g_k, w_conv, w_br, w_out,
           g_ffn, w_ffn_gate, w_ffn_up, w_ffn_down):
    batch, seq, d_model = x_prompt.shape
    dec_batch, dec_seq, _ = x_sample.shape
    depth = w_in.shape[0]
    d_conv = w_conv.shape[2]
    heads = cache_k.shape[3]
    d_att = heads * HEAD_DIM
    d_ff = w_ffn_gate.shape[2]
    d_br = w_br.shape[2]

    shared = {
        "g_mix": g_mix, "g_ffn": g_ffn,
        "w_in": w_in.astype(BF16),
        "w_br": w_br.astype(BF16).reshape(depth * w_br.shape[1], d_br, d_model),
        "w_out": w_out.astype(BF16),
        "w1": w_ffn_gate.astype(BF16), "w3": w_ffn_up.astype(BF16), "w2": w_ffn_down.astype(BF16),
        "ff_tk": min(d_ff, FF_TK),
    }
    xp = x_prompt.reshape(batch * seq, d_model)
    xs = x_sample.reshape(dec_batch * dec_seq, d_model)
    zero_state = jnp.zeros((batch, 2, d_conv), F32)
    kvp = (None, None)
    kvs = (None, None)
    cp, cs = [], []
    for l in range(depth):
        w = dict(shared,
                 g_qt=jnp.tile(g_q[l], heads).reshape(1, d_att),
                 g_kt=jnp.tile(g_k[l], heads).reshape(1, d_att),
                 w_conv=w_conv[l])
        xp, kb, vb, cn = _layer(xp, batch, seq, zero_state, None, w, l, depth, kvp, bm=1024)
        kvp = (kb, vb)
        cp.append(cn)
        xs, kb, vb, cn = _layer(xs, dec_batch, dec_seq, state_conv[l], (cache_k, cache_v), w, l, depth, kvs, bm=128)
        kvs = (kb, vb)
        cs.append(cn)

    kv_p = lambda buf: buf.reshape(depth, batch, seq, heads, HEAD_DIM)
    kv_s = lambda buf: buf.reshape(depth, dec_batch, dec_seq, heads, HEAD_DIM)
    return (xp.reshape(batch, seq, d_model), xs.reshape(dec_batch, dec_seq, d_model),
            kv_p(kvp[0]), kv_p(kvp[1]), jnp.stack(cp), kv_s(kvs[0]), kv_s(kvs[1]), jnp.stack(cs))
```

```python
import functools
import math

import jax
import jax.numpy as jnp
from jax import lax
from jax.experimental import pallas as pl
from jax.experimental.pallas import tpu as pltpu

F32 = jnp.float32
BF16 = jnp.bfloat16

HEAD_DIM = 128
EPS = 1e-6
LANES = 128
SUBLANES = 8
VMEM_LIMIT_BYTES = 56 * 1024 * 1024
LOG_WEIGHT_FLOOR = -104.0
FF_TK = 2816


def _params(n_grid):
    return pltpu.CompilerParams(dimension_semantics=("arbitrary",) * n_grid,
                                vmem_limit_bytes=VMEM_LIMIT_BYTES)


def _pick(dim, pref):
    if dim <= pref:
        return dim
    b = (pref // LANES) * LANES
    while b >= LANES:
        if dim % b == 0:
            return b
        b -= LANES
    return dim


def _head(h):
    return slice(h * HEAD_DIM, (h + 1) * HEAD_DIM)


def _rmsnorm_kernel(x_ref, g_ref, o_ref):
    x = x_ref[...]
    ms = jnp.mean(x * x, axis=-1, keepdims=True)
    o_ref[...] = ((x * lax.rsqrt(ms + EPS)) * g_ref[...]).astype(o_ref.dtype)


def _rmsnorm(x, g):
    m, d = x.shape
    tm = _pick(m, 256)
    return pl.pallas_call(
        _rmsnorm_kernel,
        grid=(m // tm,),
        in_specs=[pl.BlockSpec((tm, d), lambda i: (i, 0)),
                  pl.BlockSpec((1, d), lambda i: (0, 0))],
        out_specs=pl.BlockSpec((tm, d), lambda i: (i, 0)),
        out_shape=jax.ShapeDtypeStruct((m, d), BF16),
        compiler_params=_params(1),
        name="rmsnorm",
    )(x, g.reshape(1, d))


def _mm_kernel(*refs, n_a, n_b, pairs, n_extra, n_out, nk, k_rem, epilogue):
    a_refs = refs[:n_a]
    b_refs = refs[n_a:n_a + n_b]
    extra = refs[n_a + n_b:n_a + n_b + n_extra]
    outs = refs[n_a + n_b + n_extra:n_a + n_b + n_extra + n_out]
    acc_refs = refs[n_a + n_b + n_extra + n_out:n_a + n_b + n_extra + n_out + (len(pairs) if nk > 1 else 0)]

    def dots(kk=None):
        if kk is None:
            return [jnp.dot(a_refs[ia][...], b_refs[ib][...], preferred_element_type=F32) for ia, ib in pairs]
        return [jnp.dot(a_refs[ia][:, :kk], b_refs[ib][:kk, :], preferred_element_type=F32) for ia, ib in pairs]

    if nk == 1:
        epilogue(dots(), extra, outs)
        return

    k = pl.program_id(2)

    @pl.when(k == 0)
    def _():
        for r, d in zip(acc_refs, dots()):
            r[...] = d

    if nk > 2:
        @pl.when(jnp.logical_and(k > 0, k < nk - 1))
        def _():
            for r, d in zip(acc_refs, dots()):
                r[...] += d

    @pl.when(k == nk - 1)
    def _():
        epilogue([r[...] + d for r, d in zip(acc_refs, dots(k_rem))], extra, outs)


def _matmul(a_list, b_list, pairs, extras, outs, epilogue, *, n_cols, bm, bn, tk=None, cast=(), name):
    m, kdim = a_list[0].shape
    bm = _pick(m, bm)
    offs = [off for _, _, off in b_list] + [off for _, _, off in extras]
    if extras or any(offs):
        bn = _pick(math.gcd(n_cols, *offs), bn)
    tk = kdim if tk is None else tk
    nk = pl.cdiv(kdim, tk)
    k_rem = kdim - (nk - 1) * tk
    assert m % bm == 0 and k_rem % LANES == 0 and (nk == 1) == (tk == kdim)
    nj = pl.cdiv(n_cols, bn)
    nmb = m // bm

    in_specs = [pl.BlockSpec((bm, tk), lambda j, i, k: (i, k)) for _ in a_list]
    args = list(a_list)
    for arr, layer, off in b_list:
        assert off % bn == 0
        if layer is None:
            in_specs.append(pl.BlockSpec((tk, bn), lambda j, i, k, o=off // bn: (k, j + o)))
        else:
            in_specs.append(pl.BlockSpec((None, tk, bn), lambda j, i, k, o=off // bn, l=layer: (l, k, j + o)))
        args.append(arr)
    for arr, kind, off in extras:
        assert off % bn == 0
        if kind == "tile":
            in_specs.append(pl.BlockSpec((bm, bn), lambda j, i, k, o=off // bn: (i, j + o)))
        else:
            in_specs.append(pl.BlockSpec((1, bn), lambda j, i, k, o=off // bn: (0, j + o)))
        args.append(arr)
    n_main = len(args)
    out_specs, out_shape, aliases = [], [], {}
    for oi, o in enumerate(outs):
        if isinstance(o, tuple):
            _, dt, buf, depth, layer = o
            assert bn == SUBLANES * HEAD_DIM and n_cols % bn == 0
            out_specs.append(pl.BlockSpec((bm, None, SUBLANES, HEAD_DIM),
                                          lambda j, i, k, r=layer * nmb: (i + r, j, 0, 0)))
            out_shape.append(jax.ShapeDtypeStruct((depth * m, nj, SUBLANES, HEAD_DIM), dt))
            if buf is not None:
                in_specs.append(pl.BlockSpec(memory_space=pl.ANY))
                aliases[len(args)] = oi
                args.append(buf)
        else:
            out_specs.append(pl.BlockSpec((bm, bn), lambda j, i, k: (i, j)))
            out_shape.append(jax.ShapeDtypeStruct((m, n_cols), o))
    n_alias = len(aliases)
    bf16_rows = 2 * SUBLANES
    for src, layer in cast:
        _, r, c = src.shape
        rb = pl.cdiv(pl.cdiv(r, nj * nmb * nk), bf16_rows) * bf16_rows
        last = pl.cdiv(r, rb) - 1
        in_specs.append(pl.BlockSpec((None, rb, c), lambda j, i, k, l=layer, e=last:
                                     (l, jnp.minimum((j * nmb + i) * nk + k, e), 0)))
        args.append(src)
        out_specs.append(pl.BlockSpec((rb, c), lambda j, i, k, e=last:
                                      (jnp.minimum((j * nmb + i) * nk + k, e), 0)))
        out_shape.append(jax.ShapeDtypeStruct((r, c), BF16))
    n_cast = len(cast)
    scratch = [pltpu.VMEM((bm, bn), F32) for _ in pairs] if nk > 1 else []

    def body(*refs):
        cast_in = refs[n_main + n_alias:n_main + n_alias + n_cast]
        first_out = n_main + n_alias + n_cast
        cast_out = refs[first_out + len(outs):first_out + len(outs) + n_cast]
        for src_ref, dst_ref in zip(cast_in, cast_out):
            dst_ref[...] = src_ref[...].astype(dst_ref.dtype)
        main = refs[:n_main] + refs[first_out:first_out + len(outs)] + refs[first_out + len(outs) + n_cast:]
        _mm_kernel(*main, n_a=len(a_list), n_b=len(b_list), pairs=tuple(pairs), n_extra=len(extras),
                   n_out=len(outs), nk=nk, k_rem=k_rem, epilogue=epilogue)

    return pl.pallas_call(
        body,
        grid=(nj, nmb, nk),
        in_specs=in_specs,
        out_specs=out_specs,
        out_shape=out_shape,
        scratch_shapes=scratch,
        input_output_aliases=aliases,
        compiler_params=_params(3),
        name=name,
    )(*args)


def _store_head(o_ref, h, y):
    if len(o_ref.shape) == 3:
        rows = o_ref.shape[0]
        o_ref.reshape(rows * SUBLANES, HEAD_DIM)[pl.ds(h, rows, stride=SUBLANES), :] = y.astype(o_ref.dtype)
    else:
        o_ref[:, _head(h)] = y.astype(o_ref.dtype)


def _epi_store(accs, extra, outs):
    for o in outs:
        if len(o.shape) == 3:
            for h in range(accs[0].shape[1] // HEAD_DIM):
                _store_head(o, h, accs[0][:, _head(h)])
        else:
            o[...] = accs[0].astype(o.dtype)


def _epi_headnorm(accs, extra, outs):
    x = accs[0]
    g = extra[0][...]
    for h in range(x.shape[1] // HEAD_DIM):
        xh = x[:, _head(h)]
        ms = jnp.mean(xh * xh, axis=-1, keepdims=True)
        y = (xh * lax.rsqrt(ms + EPS)) * g[:, _head(h)]
        for o in outs:
            _store_head(o, h, y)


def _epi_merge(accs, extra, outs):
    ga = jax.nn.sigmoid(extra[0][...])
    gb = jax.nn.sigmoid(extra[1][...])
    outs[0][...] = (ga * accs[0] + gb * accs[1]).astype(outs[0].dtype)


def _epi_resid(accs, extra, outs):
    outs[0][...] = extra[0][...] + accs[0]


def _epi_swiglu(accs, extra, outs):
    g = accs[0]
    outs[0][...] = ((g * jax.nn.sigmoid(g)) * accs[1]).astype(outs[0].dtype)


def _conv_kernel(bg_ref, cg_ref, xc_ref, cgp_ref, xcp_ref, init_ref, w_ref, ob_ref, cn_ref, buf_ref, *, ts):
    s = pl.program_id(2)
    u = cg_ref[...] * xc_ref[...]
    prev_u = cgp_ref[SUBLANES - 2:, :] * xcp_ref[SUBLANES - 2:, :]
    prev = jnp.where(s == 0, init_ref[...], prev_u)
    buf_ref[SUBLANES - 2:SUBLANES, :] = prev
    buf_ref[SUBLANES:, :] = u
    u1 = buf_ref[pl.ds(SUBLANES - 1, ts), :]
    u2 = buf_ref[pl.ds(SUBLANES - 2, ts), :]
    w = w_ref[...]
    y = u2 * w[0:1, :] + u1 * w[1:2, :] + u * w[2:3, :]
    ob_ref[...] = (bg_ref[...] * y).astype(ob_ref.dtype)
    cn_ref[...] = buf_ref[pl.ds(ts + SUBLANES - 2, 2), :]


def _short_conv(bcx, init, w, batch, seq):
    c = w.shape[1]
    ts = _pick(seq, 512)
    tc = _pick(c, 512)
    ncb = c // tc
    x3 = bcx.reshape(batch, seq, 3 * c)
    sub = ts // SUBLANES

    def cur(region):
        return pl.BlockSpec((None, ts, tc), lambda b, j, s, r=region: (b, s, j + r * ncb))

    def prev(region):
        return pl.BlockSpec((None, SUBLANES, tc),
                            lambda b, j, s, r=region: (b, jnp.maximum(s * sub - 1, 0), j + r * ncb))

    ob, cn = pl.pallas_call(
        functools.partial(_conv_kernel, ts=ts),
        grid=(batch, ncb, seq // ts),
        in_specs=[cur(0), cur(1), cur(2), prev(1), prev(2),
                  pl.BlockSpec((None, 2, tc), lambda b, j, s: (b, 0, j)),
                  pl.BlockSpec((3, tc), lambda b, j, s: (0, j))],
        out_specs=[pl.BlockSpec((None, ts, tc), lambda b, j, s: (b, s, j)),
                   pl.BlockSpec((None, 2, tc), lambda b, j, s: (b, 0, j))],
        out_shape=[jax.ShapeDtypeStruct((batch, seq, c), BF16),
                   jax.ShapeDtypeStruct((batch, 2, c), F32)],
        scratch_shapes=[pltpu.VMEM((ts + SUBLANES, tc), F32)],
        compiler_params=_params(3),
        name="short_conv",
    )(x3, x3, x3, x3, x3, init, w)
    return ob.reshape(batch * seq, c), cn


def _sb_heads(qs, kbs, vbs, u_tri, carries, accs, masks, scale):
    chains = [(h, b) for h in range(len(qs)) for b in range(len(kbs[h]))]
    dot = functools.partial(jnp.dot, preferred_element_type=F32)
    z = {c: lax.dot_general(qs[c[0]], kbs[c[0]][c[1]], (((1,), (1,)), ((), ())),
                            preferred_element_type=F32) * scale for c in chains}
    sp = {c: jnp.maximum(z[c], 0.0) + jnp.log(1.0 + jnp.exp(-jnp.abs(z[c]))) for c in chains}
    lneg = {c: -sp[c] if masks[c[1]] is None else jnp.where(masks[c[1]], -sp[c], 0.0) for c in chains}
    hi = {c: lneg[c].astype(BF16) for c in chains}
    lo = {c: (lneg[c] - hi[c].astype(F32)).astype(BF16) for c in chains}
    tot = {c: jnp.sum(lneg[c], axis=1, keepdims=True) for c in chains}
    tail = {c: dot(hi[c], u_tri) + dot(lo[c], u_tri) for c in chains}
    new_carries, new_accs = [], []
    for h in range(len(qs)):
        carry, acc = carries[h], accs[h]
        for b in range(len(kbs[h])):
            c = (h, b)
            a = jnp.exp((z[c] - sp[c]) + tail[c] + carry)
            if masks[b] is not None:
                a = jnp.where(masks[b], a, 0.0)
            acc = acc + dot(a.astype(BF16), vbs[h][b])
            carry = carry + tot[c]
        new_carries.append(carry)
        new_accs.append(acc)
    return tuple(new_carries), tuple(new_accs)


def _any_weight_left(carries):
    m = carries[0]
    for c in carries[1:]:
        m = jnp.maximum(m, c)
    return (jnp.max(m) > LOG_WEIGHT_FLOOR).astype(jnp.int32)


def _sba_prompt_kernel(q_ref, k_ref, v_ref, u_ref, o_ref, *, t, hp, scale):
    i = pl.program_id(2)
    u_tri = u_ref[...]
    diff = lax.broadcasted_iota(jnp.int32, (t, t), 1) - lax.broadcasted_iota(jnp.int32, (t, t), 0)
    qs = [q_ref[:, _head(h)] for h in range(hp)]

    def kv(j):
        st = pl.multiple_of(j * t, t)
        return ([k_ref[pl.ds(st, t), _head(h)] for h in range(hp)],
                [v_ref[pl.ds(st, t), _head(h)] for h in range(hp)])

    first = jnp.maximum(i - 1, 0)
    k_lo, v_lo = kv(first)
    k_hi, v_hi = kv(first + 1)
    masks = [diff < (i - first - 1) * t, diff < (i - first) * t]
    zero_c = tuple(jnp.zeros((t, 1), F32) for _ in range(hp))
    zero_a = tuple(jnp.zeros((t, HEAD_DIM), F32) for _ in range(hp))
    carries, accs = _sb_heads(qs, [[k_hi[h], k_lo[h]] for h in range(hp)], [[v_hi[h], v_lo[h]] for h in range(hp)],
                              u_tri, zero_c, zero_a, masks, scale)

    def cond(c):
        return jnp.logical_and(c[0] >= 0, c[1] > 0)

    def body(c):
        j, _, carries, accs = c
        kj, vj = kv(j)
        carries, accs = _sb_heads(qs, [[x] for x in kj], [[x] for x in vj], u_tri, carries, accs, [None], scale)
        return j - 1, _any_weight_left(carries), carries, accs

    _, _, carries, accs = lax.while_loop(cond, body, (first - 1, _any_weight_left(carries), carries, accs))
    for h in range(hp):
        o_ref[:, _head(h)] = accs[h].astype(o_ref.dtype)


def _tri(t):
    r = lax.broadcasted_iota(jnp.int32, (t, t), 0)
    c = lax.broadcasted_iota(jnp.int32, (t, t), 1)
    return (r > c).astype(BF16)


def _sba_prompt(q, k, v, batch, seq):
    d_att = q.shape[1]
    heads = d_att // HEAD_DIM
    hp = 2 if heads % 2 == 0 else 1
    t = _pick(seq, 256)
    assert seq >= 2 * t
    q3, k3, v3 = (a.reshape(batch, seq, d_att) for a in (q, k, v))
    out = pl.pallas_call(
        functools.partial(_sba_prompt_kernel, t=t, hp=hp, scale=HEAD_DIM ** -0.5),
        grid=(batch, heads // hp, seq // t),
        in_specs=[pl.BlockSpec((None, t, hp * HEAD_DIM), lambda b, g, i: (b, i, g)),
                  pl.BlockSpec((None, seq, hp * HEAD_DIM), lambda b, g, i: (b, 0, g)),
                  pl.BlockSpec((None, seq, hp * HEAD_DIM), lambda b, g, i: (b, 0, g)),
                  pl.BlockSpec((t, t), lambda b, g, i: (0, 0))],
        out_specs=pl.BlockSpec((None, t, hp * HEAD_DIM), lambda b, g, i: (b, i, g)),
        out_shape=jax.ShapeDtypeStruct((batch, seq, d_att), BF16),
        compiler_params=_params(3),
        name="sba_prompt",
    )(q3, k3, v3, _tri(t))
    return out.reshape(batch * seq, d_att)


def _sba_sample_kernel(q_ref, kn_ref, vn_ref, kc_ref, vc_ref, u_ref, o_ref, carry_ref, acc_ref, alive_ref,
                       *, nq, tk, heads, scale):
    s = pl.program_id(1)
    u_tri = u_ref[...]

    hs = range(heads)
    qs = [q_ref[:, _head(h)] for h in hs]

    def step(kbs, vbs, carries, accs, mask):
        carries, accs = _sb_heads(qs, [[x] for x in kbs], [[x] for x in vbs], u_tri, carries, accs, [mask], scale)
        for h in hs:
            carry_ref[h] = carries[h]
            acc_ref[h] = accs[h]
        alive_ref[0] = _any_weight_left(carries)

    @pl.when(s == 0)
    def _():
        row = lax.broadcasted_iota(jnp.int32, (nq, tk), 0)
        col = lax.broadcasted_iota(jnp.int32, (nq, tk), 1)
        step([kn_ref[:, _head(h)] for h in hs], [vn_ref[:, _head(h)] for h in hs],
             [jnp.zeros((nq, 1), F32) for _ in hs], [jnp.zeros((nq, HEAD_DIM), F32) for _ in hs], col < row)

    @pl.when(alive_ref[0] > 0)
    def _():
        step([kc_ref[pl.ds(h, tk, stride=heads), :].astype(BF16) for h in hs],
             [vc_ref[pl.ds(h, tk, stride=heads), :].astype(BF16) for h in hs],
             [carry_ref[h] for h in hs], [acc_ref[h] for h in hs], None)

    for h in range(heads):
        o_ref[:, _head(h)] = acc_ref[h].astype(o_ref.dtype)


def _sba_sample(q, k_new, v_new, cache_k, cache_v, layer, batch, nq):
    d_att = q.shape[1]
    heads = d_att // HEAD_DIM
    depth, _, past = cache_k.shape[:3]
    tk = _pick(past, 128)
    assert nq <= tk and past % tk == 0
    nkb = past // tk
    q3 = q.reshape(batch, nq, d_att)
    pad = lambda a: jnp.pad(a.reshape(batch, nq, d_att), ((0, 0), (0, tk - nq), (0, 0)))
    kc = cache_k.reshape(depth, batch, past * heads, HEAD_DIM)
    vc = cache_v.reshape(depth, batch, past * heads, HEAD_DIM)
    cache_spec = pl.BlockSpec((None, None, tk * heads, HEAD_DIM), lambda b, s: (layer, b, nkb - 1 - s, 0))
    out = pl.pallas_call(
        functools.partial(_sba_sample_kernel, nq=nq, tk=tk, heads=heads, scale=HEAD_DIM ** -0.5),
        grid=(batch, nkb),
        in_specs=[pl.BlockSpec((None, nq, d_att), lambda b, s: (b, 0, 0)),
                  pl.BlockSpec((None, tk, d_att), lambda b, s: (b, 0, 0)),
                  pl.BlockSpec((None, tk, d_att), lambda b, s: (b, 0, 0)),
                  cache_spec, cache_spec,
                  pl.BlockSpec((tk, tk), lambda b, s: (0, 0))],
        out_specs=pl.BlockSpec((None, nq, d_att), lambda b, s: (b, 0, 0)),
        out_shape=jax.ShapeDtypeStruct((batch, nq, d_att), BF16),
        scratch_shapes=[pltpu.VMEM((heads, nq, 1), F32), pltpu.VMEM((heads, nq, HEAD_DIM), F32),
                        pltpu.SMEM((1,), jnp.int32)],
        compiler_params=_params(2),
        name="sba_sample",
    )(q3, pad(k_new), pad(v_new), kc, vc, _tri(tk))
    return out.reshape(batch * nq, d_att)


def _layer(x, batch, seq, conv_init, cache, w, src, layer, depth, kv_bufs, bm):
    d_model = x.shape[1]
    d_att = w["g_qt"].shape[1]
    d_conv = w["w_conv"].shape[1]
    w_in = w["w_in"]
    bn = SUBLANES * HEAD_DIM

    def mm(*args, makes=(), **kw):
        cast = [(src[name], layer) for name in makes] if src is not None else []
        res = _matmul(*args, cast=cast, **kw)
        for name, arr in zip(makes if src is not None else (), res[len(res) - len(cast):]):
            w[name] = arr
        return res[:len(res) - len(cast)]

    h = _rmsnorm(x, w["g_mix"][layer])
    (q,) = mm([h], [(w_in, None, 0)], [(0, 0)], [(w["g_qt"], "row", 0)], [BF16], _epi_headnorm,
              n_cols=d_att, bm=bm, bn=bn, name="proj_q", makes=("w_br",))
    k_buf, k16 = mm([h], [(w_in, None, d_att)], [(0, 0)], [(w["g_kt"], "row", 0)],
                    [("heads", F32, kv_bufs[0], depth, layer), BF16], _epi_headnorm,
                    n_cols=d_att, bm=bm, bn=bn, name="proj_k")
    v_buf, v16 = mm([h], [(w_in, None, 2 * d_att)], [(0, 0)], [],
                    [("heads", F32, kv_bufs[1], depth, layer), BF16], _epi_store,
                    n_cols=d_att, bm=bm, bn=bn, name="proj_v")
    (bcx,) = mm([h], [(w_in, None, 3 * d_att)], [(0, 0)], [], [F32], _epi_store,
                n_cols=3 * d_conv, bm=bm, bn=bn, name="proj_conv", makes=("w_out",))
    (gl,) = mm([h], [(w_in, None, 3 * d_att + 3 * d_conv)], [(0, 0)], [], [F32], _epi_store,
               n_cols=2 * d_model, bm=bm, bn=bn, name="proj_gate", makes=("w1",))
    if cache is None:
        o_a = _sba_prompt(q, k16, v16, batch, seq)
    else:
        o_a = _sba_sample(q, k16, v16, cache[0], cache[1], layer, batch, seq)
    o_b, conv_new = _short_conv(bcx, conv_init, w["w_conv"], batch, seq)
    w_br = w["w_br"].reshape(2, -1, d_model)
    (m,) = mm([o_a, o_b], [(w_br, 0, 0), (w_br, 1, 0)], [(0, 0), (1, 1)],
              [(gl, "tile", 0), (gl, "tile", d_model)], [BF16], _epi_merge,
              n_cols=d_model, bm=bm, bn=512, name="merge", makes=("w3",))
    (x,) = mm([m], [(w["w_out"], None, 0)], [(0, 0)], [(x, "tile", 0)], [F32], _epi_resid,
              n_cols=d_model, bm=bm, bn=bn, name="out_proj")
    h2 = _rmsnorm(x, w["g_ffn"][layer])
    d_ff = w["w1"].shape[1]
    (hid,) = mm([h2], [(w["w1"], None, 0), (w["w3"], None, 0)], [(0, 0), (0, 1)], [], [BF16], _epi_swiglu,
                n_cols=d_ff, bm=bm, bn=512, name="ffn_up", makes=("w2",))
    next_w_in = None
    if src is not None and layer + 1 < depth:
        x, next_w_in = _matmul([hid], [(w["w2"], None, 0)], [(0, 0)], [(x, "tile", 0)], [F32], _epi_resid,
                               n_cols=d_model, bm=bm, bn=bn, tk=min(d_ff, FF_TK), name="ffn_down",
                               cast=[(src["w_in"], layer + 1)])
    else:
        (x,) = _matmul([hid], [(w["w2"], None, 0)], [(0, 0)], [(x, "tile", 0)], [F32], _epi_resid,
                       n_cols=d_model, bm=bm, bn=bn, tk=min(d_ff, FF_TK), name="ffn_down")
    return x, k_buf, v_buf, conv_new, next_w_in


def kernel(x_prompt, x_sample, cache_k, cache_v, state_conv, g_mix, w_in, g_q, g_k, w_conv, w_br, w_out,
           g_ffn, w_ffn_gate, w_ffn_up, w_ffn_down):
    batch, seq, d_model = x_prompt.shape
    dec_batch, dec_seq, _ = x_sample.shape
    depth = w_in.shape[0]
    d_conv = w_conv.shape[2]
    heads = cache_k.shape[3]
    d_att = heads * HEAD_DIM

    src = {"w_in": w_in, "w_br": w_br.reshape(depth, -1, d_model), "w_out": w_out,
           "w1": w_ffn_gate, "w3": w_ffn_up, "w2": w_ffn_down}
    xp = x_prompt.reshape(batch * seq, d_model)
    xs = x_sample.reshape(dec_batch * dec_seq, d_model)
    zero_state = jnp.zeros((batch, 2, d_conv), F32)
    kvp = (None, None)
    kvs = (None, None)
    cp, cs = [], []
    w_in_l = w_in[0].astype(BF16)
    for l in range(depth):
        w = {"g_mix": g_mix, "g_ffn": g_ffn, "w_in": w_in_l, "w_conv": w_conv[l],
             "g_qt": jnp.tile(g_q[l], heads).reshape(1, d_att),
             "g_kt": jnp.tile(g_k[l], heads).reshape(1, d_att)}
        xp, kb, vb, cn, w_in_l = _layer(xp, batch, seq, zero_state, None, w, src, l, depth, kvp, bm=1024)
        kvp = (kb, vb)
        cp.append(cn)
        xs, kb, vb, cn, _ = _layer(xs, dec_batch, dec_seq, state_conv[l], (cache_k, cache_v), w, None, l, depth,
                                   kvs, bm=128)
        kvs = (kb, vb)
        cs.append(cn)

    kv_p = lambda buf: buf.reshape(depth, batch, seq, heads, HEAD_DIM)
    kv_s = lambda buf: buf.reshape(depth, dec_batch, dec_seq, heads, HEAD_DIM)
    return (xp.reshape(batch, seq, d_model), xs.reshape(dec_batch, dec_seq, d_model),
            kv_p(kvp[0]), kv_p(kvp[1]), jnp.stack(cp), kv_s(kvs[0]), kv_s(kvs[1]), jnp.stack(cs))
```

```python
import functools
import math

import jax
import jax.numpy as jnp
from jax import lax
from jax.experimental import pallas as pl
from jax.experimental.pallas import tpu as pltpu

F32 = jnp.float32
BF16 = jnp.bfloat16

HEAD_DIM = 128
EPS = 1e-6
LANES = 128
SUBLANES = 8
VMEM_LIMIT_BYTES = 56 * 1024 * 1024
LOG_WEIGHT_FLOOR = -104.0
FF_TK = 2816


def _params(n_grid):
    return pltpu.CompilerParams(dimension_semantics=("arbitrary",) * n_grid,
                                vmem_limit_bytes=VMEM_LIMIT_BYTES)


def _pick(dim, pref):
    if dim <= pref:
        return dim
    b = (pref // LANES) * LANES
    while b >= LANES:
        if dim % b == 0:
            return b
        b -= LANES
    return dim


def _head(h):
    return slice(h * HEAD_DIM, (h + 1) * HEAD_DIM)


def _rmsnorm_kernel(x_ref, g_ref, o_ref):
    x = x_ref[...]
    ms = jnp.mean(x * x, axis=-1, keepdims=True)
    o_ref[...] = ((x * lax.rsqrt(ms + EPS)) * g_ref[...]).astype(o_ref.dtype)


def _rmsnorm(x, g):
    m, d = x.shape
    tm = _pick(m, 512)
    return pl.pallas_call(
        _rmsnorm_kernel,
        grid=(m // tm,),
        in_specs=[pl.BlockSpec((tm, d), lambda i: (i, 0)),
                  pl.BlockSpec((1, d), lambda i: (0, 0))],
        out_specs=pl.BlockSpec((tm, d), lambda i: (i, 0)),
        out_shape=jax.ShapeDtypeStruct((m, d), BF16),
        compiler_params=_params(1),
        name="rmsnorm",
    )(x, g.reshape(1, d))


def _mm_kernel(*refs, n_a, n_b, pairs, n_extra, n_out, nk, k_rem, n_split, epilogue):
    a_refs = refs[:n_a]
    b_refs = refs[n_a:n_a + n_b]
    extra = refs[n_a + n_b:n_a + n_b + n_extra]
    outs = refs[n_a + n_b + n_extra:n_a + n_b + n_extra + n_out]
    n_acc = len(pairs) if nk > 1 else 0
    acc_refs = refs[n_a + n_b + n_extra + n_out:n_a + n_b + n_extra + n_out + n_acc]
    own_scratch = refs[n_a + n_b + n_extra + n_out + n_acc:]
    if own_scratch:
        epilogue = functools.partial(epilogue, scratch=own_scratch)

    def dots(kk=None):
        if kk is None:
            return [jnp.dot(a_refs[ia][...], b_refs[ib][...], preferred_element_type=F32) for ia, ib in pairs]
        return [jnp.dot(a_refs[ia][:, :kk], b_refs[ib][:kk, :], preferred_element_type=F32) for ia, ib in pairs]

    if nk == 1 and n_split > 1:
        cw = b_refs[0].shape[1] // n_split
        for c in range(n_split):
            cols = slice(c * cw, (c + 1) * cw)
            accs = [jnp.dot(a_refs[ia][...], b_refs[ib][:, cols], preferred_element_type=F32) for ia, ib in pairs]
            epilogue(accs, [e.at[:, cols] for e in extra], [o.at[:, cols] for o in outs])
        return
    if nk == 1:
        epilogue(dots(), extra, outs)
        return

    k = pl.program_id(2)

    @pl.when(k == 0)
    def _():
        for r, d in zip(acc_refs, dots()):
            r[...] = d

    if nk > 2:
        @pl.when(jnp.logical_and(k > 0, k < nk - 1))
        def _():
            for r, d in zip(acc_refs, dots()):
                r[...] += d

    @pl.when(k == nk - 1)
    def _():
        epilogue([r[...] + d for r, d in zip(acc_refs, dots(k_rem))], extra, outs)


def _matmul(a_list, b_list, pairs, extras, outs, epilogue, *, n_cols, bm, bn, tk=None, cast=(), n_split=1,
            seq_blocks=1, scratch_rows=(), name):
    m, kdim = a_list[0].shape
    bm = _pick(m, bm)
    offs = [off for _, _, off in b_list] + [off for _, _, off in extras]
    if extras or any(offs):
        bn = _pick(math.gcd(n_cols, *offs), bn)
    tk = kdim if tk is None else tk
    nk = pl.cdiv(kdim, tk)
    k_rem = kdim - (nk - 1) * tk
    assert m % bm == 0 and k_rem % LANES == 0 and (nk == 1) == (tk == kdim)
    nj = pl.cdiv(n_cols, bn)
    nmb = m // bm

    in_specs = [pl.BlockSpec((bm, tk), lambda j, i, k: (i, k)) for _ in a_list]
    args = list(a_list)
    for arr, layer, off in b_list:
        assert off % bn == 0
        if layer is None:
            in_specs.append(pl.BlockSpec((tk, bn), lambda j, i, k, o=off // bn: (k, j + o)))
        else:
            in_specs.append(pl.BlockSpec((None, tk, bn), lambda j, i, k, o=off // bn, l=layer: (l, k, j + o)))
        args.append(arr)
    for arr, kind, off in extras:
        assert off % bn == 0
        if kind == "tile":
            in_specs.append(pl.BlockSpec((bm, bn), lambda j, i, k, o=off // bn: (i, j + o)))
        elif kind == "state":
            in_specs.append(pl.BlockSpec((None, arr.shape[1], bn),
                                         lambda j, i, k, o=off // bn: (i // seq_blocks, 0, j + o)))
        else:
            in_specs.append(pl.BlockSpec((arr.shape[0], bn), lambda j, i, k, o=off // bn: (0, j + o)))
        args.append(arr)
    n_main = len(args)
    out_specs, out_shape, aliases = [], [], {}
    for oi, o in enumerate(outs):
        if isinstance(o, tuple) and o[0] == "state":
            _, dt, rows = o
            out_specs.append(pl.BlockSpec((None, rows, bn), lambda j, i, k: (i // seq_blocks, 0, j)))
            out_shape.append(jax.ShapeDtypeStruct((nmb // seq_blocks, rows, n_cols), dt))
        elif isinstance(o, tuple):
            _, dt, buf, depth, layer = o
            assert bn == SUBLANES * HEAD_DIM and n_cols % bn == 0
            out_specs.append(pl.BlockSpec((bm, None, SUBLANES, HEAD_DIM),
                                          lambda j, i, k, r=layer * nmb: (i + r, j, 0, 0)))
            out_shape.append(jax.ShapeDtypeStruct((depth * m, nj, SUBLANES, HEAD_DIM), dt))
            if buf is not None:
                in_specs.append(pl.BlockSpec(memory_space=pl.ANY))
                aliases[len(args)] = oi
                args.append(buf)
        else:
            out_specs.append(pl.BlockSpec((bm, bn), lambda j, i, k: (i, j)))
            out_shape.append(jax.ShapeDtypeStruct((m, n_cols), o))
    n_alias = len(aliases)
    bf16_rows = 2 * SUBLANES
    for src, layer in cast:
        _, r, c = src.shape
        rb = pl.cdiv(pl.cdiv(r, nj * nmb * nk), bf16_rows) * bf16_rows
        last = pl.cdiv(r, rb) - 1
        in_specs.append(pl.BlockSpec((None, rb, c), lambda j, i, k, l=layer, e=last:
                                     (l, jnp.minimum((j * nmb + i) * nk + k, e), 0)))
        args.append(src)
        out_specs.append(pl.BlockSpec((rb, c), lambda j, i, k, e=last:
                                      (jnp.minimum((j * nmb + i) * nk + k, e), 0)))
        out_shape.append(jax.ShapeDtypeStruct((r, c), BF16))
    n_cast = len(cast)
    scratch = [pltpu.VMEM((bm, bn), F32) for _ in pairs] if nk > 1 else []
    scratch += [pltpu.VMEM((rows + (bm if plus_bm else 0), bn), F32) for rows, plus_bm in scratch_rows]

    def body(*refs):
        cast_in = refs[n_main + n_alias:n_main + n_alias + n_cast]
        first_out = n_main + n_alias + n_cast
        cast_out = refs[first_out + len(outs):first_out + len(outs) + n_cast]
        for src_ref, dst_ref in zip(cast_in, cast_out):
            dst_ref[...] = src_ref[...].astype(dst_ref.dtype)
        main = refs[:n_main] + refs[first_out:first_out + len(outs)] + refs[first_out + len(outs) + n_cast:]
        _mm_kernel(*main, n_a=len(a_list), n_b=len(b_list), pairs=tuple(pairs), n_extra=len(extras),
                   n_out=len(outs), nk=nk, k_rem=k_rem, n_split=n_split, epilogue=epilogue)

    return pl.pallas_call(
        body,
        grid=(nj, nmb, nk),
        in_specs=in_specs,
        out_specs=out_specs,
        out_shape=out_shape,
        scratch_shapes=scratch,
        input_output_aliases=aliases,
        compiler_params=_params(3),
        name=name,
    )(*args)


def _store_head(o_ref, h, y):
    if len(o_ref.shape) == 3:
        rows = o_ref.shape[0]
        o_ref.reshape(rows * SUBLANES, HEAD_DIM)[pl.ds(h, rows, stride=SUBLANES), :] = y.astype(o_ref.dtype)
    else:
        o_ref[:, _head(h)] = y.astype(o_ref.dtype)


def _epi_store(accs, extra, outs):
    for o in outs:
        if len(o.shape) == 3:
            for h in range(accs[0].shape[1] // HEAD_DIM):
                _store_head(o, h, accs[0][:, _head(h)])
        else:
            o[...] = accs[0].astype(o.dtype)


def _epi_headnorm(accs, extra, outs):
    x = accs[0]
    g = extra[0][...]
    for h in range(x.shape[1] // HEAD_DIM):
        xh = x[:, _head(h)]
        ms = jnp.mean(xh * xh, axis=-1, keepdims=True)
        y = (xh * lax.rsqrt(ms + EPS)) * g[:, _head(h)]
        for o in outs:
            _store_head(o, h, y)


def _epi_merge(accs, extra, outs):
    ga = jax.nn.sigmoid(extra[0][...])
    gb = jax.nn.sigmoid(extra[1][...])
    outs[0][...] = (ga * accs[0] + gb * accs[1]).astype(outs[0].dtype)


def _epi_conv(accs, extra, outs, *, scratch, seq_blocks):
    bg, cg, xc = accs
    w = extra[0][...]
    buf_ref, carry_ref = scratch
    rows = bg.shape[0]
    u = cg * xc
    starts_sequence = pl.program_id(1) % seq_blocks == 0
    buf_ref[SUBLANES - 2:SUBLANES, :] = jnp.where(starts_sequence, extra[1][...], carry_ref[...])
    buf_ref[SUBLANES:, :] = u
    u1 = buf_ref[pl.ds(SUBLANES - 1, rows), :]
    u2 = buf_ref[pl.ds(SUBLANES - 2, rows), :]
    y = u2 * w[0:1, :] + u1 * w[1:2, :] + u * w[2:3, :]
    outs[0][...] = (bg * y).astype(outs[0].dtype)
    last2 = buf_ref[pl.ds(rows + SUBLANES - 2, 2), :]
    carry_ref[...] = last2
    outs[1][...] = last2


def _epi_resid(accs, extra, outs):
    outs[0][...] = extra[0][...] + accs[0]


def _epi_swiglu(accs, extra, outs):
    g = accs[0]
    outs[0][...] = ((g * jax.nn.sigmoid(g)) * accs[1]).astype(outs[0].dtype)


def _conv_kernel(bg_ref, cg_ref, xc_ref, cgp_ref, xcp_ref, init_ref, w_ref, ob_ref, cn_ref, buf_ref, *, ts):
    s = pl.program_id(2)
    u = cg_ref[...] * xc_ref[...]
    prev_u = cgp_ref[SUBLANES - 2:, :] * xcp_ref[SUBLANES - 2:, :]
    prev = jnp.where(s == 0, init_ref[...], prev_u)
    buf_ref[SUBLANES - 2:SUBLANES, :] = prev
    buf_ref[SUBLANES:, :] = u
    u1 = buf_ref[pl.ds(SUBLANES - 1, ts), :]
    u2 = buf_ref[pl.ds(SUBLANES - 2, ts), :]
    w = w_ref[...]
    y = u2 * w[0:1, :] + u1 * w[1:2, :] + u * w[2:3, :]
    ob_ref[...] = (bg_ref[...] * y).astype(ob_ref.dtype)
    cn_ref[...] = buf_ref[pl.ds(ts + SUBLANES - 2, 2), :]


def _short_conv(bcx, init, w, batch, seq):
    c = w.shape[1]
    ts = _pick(seq, 1024)
    tc = _pick(c, 1024)
    ncb = c // tc
    x3 = bcx.reshape(batch, seq, 3 * c)
    sub = ts // SUBLANES

    def cur(region):
        return pl.BlockSpec((None, ts, tc), lambda b, j, s, r=region: (b, s, j + r * ncb))

    def prev(region):
        return pl.BlockSpec((None, SUBLANES, tc),
                            lambda b, j, s, r=region: (b, jnp.maximum(s * sub - 1, 0), j + r * ncb))

    ob, cn = pl.pallas_call(
        functools.partial(_conv_kernel, ts=ts),
        grid=(batch, ncb, seq // ts),
        in_specs=[cur(0), cur(1), cur(2), prev(1), prev(2),
                  pl.BlockSpec((None, 2, tc), lambda b, j, s: (b, 0, j)),
                  pl.BlockSpec((3, tc), lambda b, j, s: (0, j))],
        out_specs=[pl.BlockSpec((None, ts, tc), lambda b, j, s: (b, s, j)),
                   pl.BlockSpec((None, 2, tc), lambda b, j, s: (b, 0, j))],
        out_shape=[jax.ShapeDtypeStruct((batch, seq, c), BF16),
                   jax.ShapeDtypeStruct((batch, 2, c), F32)],
        scratch_shapes=[pltpu.VMEM((ts + SUBLANES, tc), F32)],
        compiler_params=_params(3),
        name="short_conv",
    )(x3, x3, x3, x3, x3, init, w)
    return ob.reshape(batch * seq, c), cn


def _sb_heads(qs, kbs, vbs, u_tri, carries, accs, masks, scale):
    chains = [(h, b) for h in range(len(qs)) for b in range(len(kbs[h]))]
    dot = functools.partial(jnp.dot, preferred_element_type=F32)
    z = {c: lax.dot_general(qs[c[0]], kbs[c[0]][c[1]], (((1,), (1,)), ((), ())),
                            preferred_element_type=F32) * scale for c in chains}
    sp = {c: jnp.maximum(z[c], 0.0) + jnp.log(1.0 + jnp.exp(-jnp.abs(z[c]))) for c in chains}
    lneg = {c: -sp[c] if masks[c[1]] is None else jnp.where(masks[c[1]], -sp[c], 0.0) for c in chains}
    hi = {c: lneg[c].astype(BF16) for c in chains}
    lo = {c: (lneg[c] - hi[c].astype(F32)).astype(BF16) for c in chains}
    tot = {c: jnp.sum(lneg[c], axis=1, keepdims=True) for c in chains}
    tail = {c: dot(jnp.concatenate([hi[c], lo[c]], axis=1), u_tri) for c in chains}
    new_carries, new_accs = [], []
    for h in range(len(qs)):
        carry, acc = carries[h], accs[h]
        for b in range(len(kbs[h])):
            c = (h, b)
            a = jnp.exp((z[c] - sp[c]) + tail[c] + carry)
            if masks[b] is not None:
                a = jnp.where(masks[b], a, 0.0)
            acc = acc + dot(a.astype(BF16), vbs[h][b])
            carry = carry + tot[c]
        new_carries.append(carry)
        new_accs.append(acc)
    return tuple(new_carries), tuple(new_accs)


def _any_weight_left(carries):
    m = carries[0]
    for c in carries[1:]:
        m = jnp.maximum(m, c)
    return (jnp.max(m) > LOG_WEIGHT_FLOOR).astype(jnp.int32)


def _sba_prompt_kernel(q_ref, k_ref, v_ref, u_ref, o_ref, *, t, hp, scale):
    i = pl.program_id(2)
    u_tri = u_ref[...]
    diff = lax.broadcasted_iota(jnp.int32, (t, t), 1) - lax.broadcasted_iota(jnp.int32, (t, t), 0)
    qs = [q_ref[:, _head(h)] for h in range(hp)]

    def kv(j):
        st = pl.multiple_of(j * t, t)
        return ([k_ref[pl.ds(st, t), _head(h)] for h in range(hp)],
                [v_ref[pl.ds(st, t), _head(h)] for h in range(hp)])

    first = jnp.maximum(i - 1, 0)
    k_lo, v_lo = kv(first)
    k_hi, v_hi = kv(first + 1)
    masks = [diff < (i - first - 1) * t, diff < (i - first) * t]
    zero_c = tuple(jnp.zeros((t, 1), F32) for _ in range(hp))
    zero_a = tuple(jnp.zeros((t, HEAD_DIM), F32) for _ in range(hp))
    carries, accs = _sb_heads(qs, [[k_hi[h], k_lo[h]] for h in range(hp)], [[v_hi[h], v_lo[h]] for h in range(hp)],
                              u_tri, zero_c, zero_a, masks, scale)

    def cond(c):
        return jnp.logical_and(c[0] >= 0, c[1] > 0)

    def body(c):
        j, _, carries, accs = c
        kj, vj = kv(j)
        carries, accs = _sb_heads(qs, [[x] for x in kj], [[x] for x in vj], u_tri, carries, accs, [None], scale)
        return j - 1, _any_weight_left(carries), carries, accs

    _, _, carries, accs = lax.while_loop(cond, body, (first - 1, _any_weight_left(carries), carries, accs))
    for h in range(hp):
        o_ref[:, _head(h)] = accs[h].astype(o_ref.dtype)


def _tri(t):
    r = lax.broadcasted_iota(jnp.int32, (t, t), 0)
    c = lax.broadcasted_iota(jnp.int32, (t, t), 1)
    tri = (r > c).astype(BF16)
    return jnp.concatenate([tri, tri], axis=0)


def _sba_prompt(q, k, v, batch, seq):
    d_att = q.shape[1]
    heads = d_att // HEAD_DIM
    hp = 2 if heads % 2 == 0 else 1
    t = _pick(seq, 256)
    assert seq >= 2 * t
    q3, k3, v3 = (a.reshape(batch, seq, d_att) for a in (q, k, v))
    out = pl.pallas_call(
        functools.partial(_sba_prompt_kernel, t=t, hp=hp, scale=HEAD_DIM ** -0.5),
        grid=(batch, heads // hp, seq // t),
        in_specs=[pl.BlockSpec((None, t, hp * HEAD_DIM), lambda b, g, i: (b, i, g)),
                  pl.BlockSpec((None, seq, hp * HEAD_DIM), lambda b, g, i: (b, 0, g)),
                  pl.BlockSpec((None, seq, hp * HEAD_DIM), lambda b, g, i: (b, 0, g)),
                  pl.BlockSpec((2 * t, t), lambda b, g, i: (0, 0))],
        out_specs=pl.BlockSpec((None, t, hp * HEAD_DIM), lambda b, g, i: (b, i, g)),
        out_shape=jax.ShapeDtypeStruct((batch, seq, d_att), BF16),
        compiler_params=_params(3),
        name="sba_prompt",
    )(q3, k3, v3, _tri(t))
    return out.reshape(batch * seq, d_att)


def _sba_sample_kernel(q_ref, kn_ref, vn_ref, kc_ref, vc_ref, u_ref, o_ref, carry_ref, acc_ref, alive_ref,
                       *, nq, tk, heads, scale):
    s = pl.program_id(1)
    u_tri = u_ref[...]

    hs = range(heads)
    qs = [q_ref[:, _head(h)] for h in hs]

    def step(kbs, vbs, carries, accs, mask):
        carries, accs = _sb_heads(qs, [[x] for x in kbs], [[x] for x in vbs], u_tri, carries, accs, [mask], scale)
        for h in hs:
            carry_ref[h] = carries[h]
            acc_ref[h] = accs[h]
        alive_ref[0] = _any_weight_left(carries)

    @pl.when(s == 0)
    def _():
        row = lax.broadcasted_iota(jnp.int32, (nq, tk), 0)
        col = lax.broadcasted_iota(jnp.int32, (nq, tk), 1)
        step([kn_ref[:, _head(h)] for h in hs], [vn_ref[:, _head(h)] for h in hs],
             [jnp.zeros((nq, 1), F32) for _ in hs], [jnp.zeros((nq, HEAD_DIM), F32) for _ in hs], col < row)

    @pl.when(alive_ref[0] > 0)
    def _():
        step([kc_ref[pl.ds(h, tk, stride=heads), :].astype(BF16) for h in hs],
             [vc_ref[pl.ds(h, tk, stride=heads), :].astype(BF16) for h in hs],
             [carry_ref[h] for h in hs], [acc_ref[h] for h in hs], None)

    for h in range(heads):
        o_ref[:, _head(h)] = acc_ref[h].astype(o_ref.dtype)


def _sba_sample(q, k_new, v_new, cache_k, cache_v, layer, batch, nq):
    d_att = q.shape[1]
    heads = d_att // HEAD_DIM
    depth, _, past = cache_k.shape[:3]
    tk = _pick(past, 128)
    assert nq <= tk and past % tk == 0
    nkb = past // tk
    q3 = q.reshape(batch, nq, d_att)
    pad = lambda a: jnp.pad(a.reshape(batch, nq, d_att), ((0, 0), (0, tk - nq), (0, 0)))
    kc = cache_k.reshape(depth, batch, past * heads, HEAD_DIM)
    vc = cache_v.reshape(depth, batch, past * heads, HEAD_DIM)
    cache_spec = pl.BlockSpec((None, None, tk * heads, HEAD_DIM), lambda b, s: (layer, b, nkb - 1 - s, 0))
    out = pl.pallas_call(
        functools.partial(_sba_sample_kernel, nq=nq, tk=tk, heads=heads, scale=HEAD_DIM ** -0.5),
        grid=(batch, nkb),
        in_specs=[pl.BlockSpec((None, nq, d_att), lambda b, s: (b, 0, 0)),
                  pl.BlockSpec((None, tk, d_att), lambda b, s: (b, 0, 0)),
                  pl.BlockSpec((None, tk, d_att), lambda b, s: (b, 0, 0)),
                  cache_spec, cache_spec,
                  pl.BlockSpec((2 * tk, tk), lambda b, s: (0, 0))],
        out_specs=pl.BlockSpec((None, nq, d_att), lambda b, s: (b, 0, 0)),
        out_shape=jax.ShapeDtypeStruct((batch, nq, d_att), BF16),
        scratch_shapes=[pltpu.VMEM((heads, nq, 1), F32), pltpu.VMEM((heads, nq, HEAD_DIM), F32),
                        pltpu.SMEM((1,), jnp.int32)],
        compiler_params=_params(2),
        name="sba_sample",
    )(q3, pad(k_new), pad(v_new), kc, vc, _tri(tk))
    return out.reshape(batch * nq, d_att)


def _layer(x, batch, seq, conv_init, cache, w, src, layer, depth, kv_bufs, bm):
    d_model = x.shape[1]
    d_att = w["g_qt"].shape[1]
    d_conv = w["w_conv"].shape[1]
    w_in = w["w_in"]
    bn = SUBLANES * HEAD_DIM

    def mm(*args, makes=(), **kw):
        cast = [(src[name], layer) for name in makes] if src is not None else []
        res = _matmul(*args, cast=cast, **kw)
        for name, arr in zip(makes if src is not None else (), res[len(res) - len(cast):]):
            w[name] = arr
        return res[:len(res) - len(cast)]

    h = _rmsnorm(x, w["g_mix"][layer])
    (q,) = mm([h], [(w_in, None, 0)], [(0, 0)], [(w["g_qt"], "row", 0)], [BF16], _epi_headnorm,
              n_cols=d_att, bm=bm, bn=bn, name="proj_q", makes=("w_br",))
    k_buf, k16 = mm([h], [(w_in, None, d_att)], [(0, 0)], [(w["g_kt"], "row", 0)],
                    [("heads", F32, kv_bufs[0], depth, layer), BF16], _epi_headnorm,
                    n_cols=d_att, bm=bm, bn=bn, name="proj_k")
    v_buf, v16 = mm([h], [(w_in, None, 2 * d_att)], [(0, 0)], [],
                    [("heads", F32, kv_bufs[1], depth, layer), BF16], _epi_store,
                    n_cols=d_att, bm=bm, bn=bn, name="proj_v")
    rows = _pick(x.shape[0], bm)
    fuse_conv = seq % rows == 0
    if fuse_conv:
        o_b, conv_new = mm([h], [(w_in, None, 3 * d_att + r * d_conv) for r in range(3)], [(0, 0), (0, 1), (0, 2)],
                           [(w["w_conv"], "row", 0), (conv_init, "state", 0)], [BF16, ("state", F32, 2)],
                           functools.partial(_epi_conv, seq_blocks=seq // rows),
                           n_cols=d_conv, bm=bm, bn=256, seq_blocks=seq // rows,
                           scratch_rows=[(SUBLANES, True), (2, False)], name="proj_conv", makes=("w_out",))
    else:
        (bcx,) = mm([h], [(w_in, None, 3 * d_att)], [(0, 0)], [], [F32], _epi_store,
                    n_cols=3 * d_conv, bm=bm, bn=bn, name="proj_conv", makes=("w_out",))
    (gl,) = mm([h], [(w_in, None, 3 * d_att + 3 * d_conv)], [(0, 0)], [], [F32], _epi_store,
               n_cols=2 * d_model, bm=bm, bn=bn, name="proj_gate", makes=("w1",))
    if cache is None:
        o_a = _sba_prompt(q, k16, v16, batch, seq)
    else:
        o_a = _sba_sample(q, k16, v16, cache[0], cache[1], layer, batch, seq)
    if not fuse_conv:
        o_b, conv_new = _short_conv(bcx, conv_init, w["w_conv"], batch, seq)
    w_br = w["w_br"].reshape(2, -1, d_model)
    (m,) = mm([o_a, o_b], [(w_br, 0, 0), (w_br, 1, 0)], [(0, 0), (1, 1)],
              [(gl, "tile", 0), (gl, "tile", d_model)], [BF16], _epi_merge,
              n_cols=d_model, bm=bm, bn=512, name="merge", makes=("w3",))
    (x,) = mm([m], [(w["w_out"], None, 0)], [(0, 0)], [(x, "tile", 0)], [F32], _epi_resid,
              n_cols=d_model, bm=bm, bn=bn, name="out_proj")
    h2 = _rmsnorm(x, w["g_ffn"][layer])
    d_ff = w["w1"].shape[1]
    (hid,) = mm([h2], [(w["w1"], None, 0), (w["w3"], None, 0)], [(0, 0), (0, 1)], [], [BF16], _epi_swiglu,
                n_cols=d_ff, bm=bm, bn=512, n_split=2, name="ffn_up", makes=("w2",))
    next_w_in = None
    if src is not None and layer + 1 < depth:
        x, next_w_in = _matmul([hid], [(w["w2"], None, 0)], [(0, 0)], [(x, "tile", 0)], [F32], _epi_resid,
                               n_cols=d_model, bm=bm, bn=bn, tk=min(d_ff, FF_TK), name="ffn_down",
                               cast=[(src["w_in"], layer + 1)])
    else:
        (x,) = _matmul([hid], [(w["w2"], None, 0)], [(0, 0)], [(x, "tile", 0)], [F32], _epi_resid,
                       n_cols=d_model, bm=bm, bn=bn, tk=min(d_ff, FF_TK), name="ffn_down")
    return x, k_buf, v_buf, conv_new, next_w_in


def kernel(x_prompt, x_sample, cache_k, cache_v, state_conv, g_mix, w_in, g_q, g_k, w_conv, w_br, w_out,
           g_ffn, w_ffn_gate, w_ffn_up, w_ffn_down):
    batch, seq, d_model = x_prompt.shape
    dec_batch, dec_seq, _ = x_sample.shape
    depth = w_in.shape[0]
    d_conv = w_conv.shape[2]
    heads = cache_k.shape[3]
    d_att = heads * HEAD_DIM

    src = {"w_in": w_in, "w_br": w_br.reshape(depth, -1, d_model), "w_out": w_out,
           "w1": w_ffn_gate, "w3": w_ffn_up, "w2": w_ffn_down}
    xp = x_prompt.reshape(batch * seq, d_model)
    xs = x_sample.reshape(dec_batch * dec_seq, d_model)
    zero_state = jnp.zeros((batch, 2, d_conv), F32)
    kvp = (None, None)
    kvs = (None, None)
    cp, cs = [], []
    w_in_l = w_in[0].astype(BF16)
    for l in range(depth):
        w = {"g_mix": g_mix, "g_ffn": g_ffn, "w_in": w_in_l, "w_conv": w_conv[l],
             "g_qt": jnp.tile(g_q[l], heads).reshape(1, d_att),
             "g_kt": jnp.tile(g_k[l], heads).reshape(1, d_att)}
        xp, kb, vb, cn, w_in_l = _layer(xp, batch, seq, zero_state, None, w, src, l, depth, kvp, bm=1024)
        kvp = (kb, vb)
        cp.append(cn)
        xs, kb, vb, cn, _ = _layer(xs, dec_batch, dec_seq, state_conv[l], (cache_k, cache_v), w, None, l, depth,
                                   kvs, bm=128)
        kvs = (kb, vb)
        cs.append(cn)

    kv_p = lambda buf: buf.reshape(depth, batch, seq, heads, HEAD_DIM)
    kv_s = lambda buf: buf.reshape(depth, dec_batch, dec_seq, heads, HEAD_DIM)
    return (xp.reshape(batch, seq, d_model), xs.reshape(dec_batch, dec_seq, d_model),
            kv_p(kvp[0]), kv_p(kvp[1]), jnp.stack(cp), kv_s(kvs[0]), kv_s(kvs[1]), jnp.stack(cs))
```

```python
import functools
import math

import jax
import jax.numpy as jnp
from jax import lax
from jax.experimental import pallas as pl
from jax.experimental.pallas import tpu as pltpu

F32 = jnp.float32
BF16 = jnp.bfloat16

HEAD_DIM = 128
EPS = 1e-6
LANES = 128
SUBLANES = 8
VMEM_LIMIT_BYTES = 56 * 1024 * 1024
LOG_WEIGHT_FLOOR = -104.0
FF_TK = 2816


def _params(n_grid):
    return pltpu.CompilerParams(dimension_semantics=("arbitrary",) * n_grid,
                                vmem_limit_bytes=VMEM_LIMIT_BYTES)


def _pick(dim, pref):
    if dim <= pref:
        return dim
    b = (pref // LANES) * LANES
    while b >= LANES:
        if dim % b == 0:
            return b
        b -= LANES
    return dim


def _head(h):
    return slice(h * HEAD_DIM, (h + 1) * HEAD_DIM)


def _rmsnorm_kernel(x_ref, g_ref, o_ref):
    x = x_ref[...]
    ms = jnp.mean(x * x, axis=-1, keepdims=True)
    o_ref[...] = ((x * lax.rsqrt(ms + EPS)) * g_ref[...]).astype(o_ref.dtype)


def _rmsnorm(x, g):
    m, d = x.shape
    tm = _pick(m, 512)
    return pl.pallas_call(
        _rmsnorm_kernel,
        grid=(m // tm,),
        in_specs=[pl.BlockSpec((tm, d), lambda i: (i, 0)),
                  pl.BlockSpec((1, d), lambda i: (0, 0))],
        out_specs=pl.BlockSpec((tm, d), lambda i: (i, 0)),
        out_shape=jax.ShapeDtypeStruct((m, d), BF16),
        compiler_params=_params(1),
        name="rmsnorm",
    )(x, g.reshape(1, d))


def _mm_kernel(*refs, n_a, n_b, pairs, n_extra, n_out, nk, k_rem, n_split, epilogue):
    a_refs = refs[:n_a]
    b_refs = refs[n_a:n_a + n_b]
    extra = refs[n_a + n_b:n_a + n_b + n_extra]
    outs = refs[n_a + n_b + n_extra:n_a + n_b + n_extra + n_out]
    n_acc = len(pairs) if nk > 1 else 0
    acc_refs = refs[n_a + n_b + n_extra + n_out:n_a + n_b + n_extra + n_out + n_acc]
    own_scratch = refs[n_a + n_b + n_extra + n_out + n_acc:]
    if own_scratch:
        epilogue = functools.partial(epilogue, scratch=own_scratch)

    def dots(kk=None):
        if kk is None:
            return [jnp.dot(a_refs[ia][...], b_refs[ib][...], preferred_element_type=F32) for ia, ib in pairs]
        return [jnp.dot(a_refs[ia][:, :kk], b_refs[ib][:kk, :], preferred_element_type=F32) for ia, ib in pairs]

    if nk == 1 and n_split > 1:
        cw = b_refs[0].shape[1] // n_split
        for c in range(n_split):
            cols = slice(c * cw, (c + 1) * cw)
            accs = [jnp.dot(a_refs[ia][...], b_refs[ib][:, cols], preferred_element_type=F32) for ia, ib in pairs]
            epilogue(accs, [e.at[:, cols] for e in extra], [o.at[:, cols] for o in outs])
        return
    if nk == 1:
        epilogue(dots(), extra, outs)
        return

    k = pl.program_id(2)

    @pl.when(k == 0)
    def _():
        for r, d in zip(acc_refs, dots()):
            r[...] = d

    if nk > 2:
        @pl.when(jnp.logical_and(k > 0, k < nk - 1))
        def _():
            for r, d in zip(acc_refs, dots()):
                r[...] += d

    @pl.when(k == nk - 1)
    def _():
        epilogue([r[...] + d for r, d in zip(acc_refs, dots(k_rem))], extra, outs)


def _matmul(a_list, b_list, pairs, extras, outs, epilogue, *, n_cols, bm, bn, tk=None, cast=(), n_split=1,
            seq_blocks=1, scratch_rows=(), name):
    m, kdim = a_list[0].shape
    bm = _pick(m, bm)
    offs = [off for _, _, off in b_list] + [off for _, _, off in extras]
    if extras or any(offs):
        bn = _pick(math.gcd(n_cols, *offs), bn)
    tk = kdim if tk is None else tk
    nk = pl.cdiv(kdim, tk)
    k_rem = kdim - (nk - 1) * tk
    assert m % bm == 0 and k_rem % LANES == 0 and (nk == 1) == (tk == kdim)
    nj = pl.cdiv(n_cols, bn)
    nmb = m // bm

    in_specs = [pl.BlockSpec((bm, tk), lambda j, i, k: (i, k)) for _ in a_list]
    args = list(a_list)
    for arr, layer, off in b_list:
        assert off % bn == 0
        if layer is None:
            in_specs.append(pl.BlockSpec((tk, bn), lambda j, i, k, o=off // bn: (k, j + o)))
        else:
            in_specs.append(pl.BlockSpec((None, tk, bn), lambda j, i, k, o=off // bn, l=layer: (l, k, j + o)))
        args.append(arr)
    for arr, kind, off in extras:
        assert off % bn == 0
        if kind == "tile":
            in_specs.append(pl.BlockSpec((bm, bn), lambda j, i, k, o=off // bn: (i, j + o)))
        elif kind == "state":
            in_specs.append(pl.BlockSpec((None, arr.shape[1], bn),
                                         lambda j, i, k, o=off // bn: (i // seq_blocks, 0, j + o)))
        else:
            in_specs.append(pl.BlockSpec((arr.shape[0], bn), lambda j, i, k, o=off // bn: (0, j + o)))
        args.append(arr)
    n_main = len(args)
    out_specs, out_shape, aliases = [], [], {}
    for oi, o in enumerate(outs):
        if isinstance(o, tuple) and o[0] == "state":
            _, dt, rows = o
            out_specs.append(pl.BlockSpec((None, rows, bn), lambda j, i, k: (i // seq_blocks, 0, j)))
            out_shape.append(jax.ShapeDtypeStruct((nmb // seq_blocks, rows, n_cols), dt))
        elif isinstance(o, tuple):
            _, dt, buf, depth, layer = o
            assert bn == SUBLANES * HEAD_DIM and n_cols % bn == 0
            out_specs.append(pl.BlockSpec((bm, None, SUBLANES, HEAD_DIM),
                                          lambda j, i, k, r=layer * nmb: (i + r, j, 0, 0)))
            out_shape.append(jax.ShapeDtypeStruct((depth * m, nj, SUBLANES, HEAD_DIM), dt))
            if buf is not None:
                in_specs.append(pl.BlockSpec(memory_space=pl.ANY))
                aliases[len(args)] = oi
                args.append(buf)
        else:
            out_specs.append(pl.BlockSpec((bm, bn), lambda j, i, k: (i, j)))
            out_shape.append(jax.ShapeDtypeStruct((m, n_cols), o))
    n_alias = len(aliases)
    bf16_rows = 2 * SUBLANES
    for src, layer in cast:
        _, r, c = src.shape
        rb = pl.cdiv(pl.cdiv(r, nj * nmb * nk), bf16_rows) * bf16_rows
        last = pl.cdiv(r, rb) - 1
        in_specs.append(pl.BlockSpec((None, rb, c), lambda j, i, k, l=layer, e=last:
                                     (l, jnp.minimum((j * nmb + i) * nk + k, e), 0)))
        args.append(src)
        out_specs.append(pl.BlockSpec((rb, c), lambda j, i, k, e=last:
                                      (jnp.minimum((j * nmb + i) * nk + k, e), 0)))
        out_shape.append(jax.ShapeDtypeStruct((r, c), BF16))
    n_cast = len(cast)
    scratch = [pltpu.VMEM((bm, bn), F32) for _ in pairs] if nk > 1 else []
    scratch += [pltpu.VMEM((rows + (bm if plus_bm else 0), bn), F32) for rows, plus_bm in scratch_rows]

    def body(*refs):
        cast_in = refs[n_main + n_alias:n_main + n_alias + n_cast]
        first_out = n_main + n_alias + n_cast
        cast_out = refs[first_out + len(outs):first_out + len(outs) + n_cast]
        for src_ref, dst_ref in zip(cast_in, cast_out):
            dst_ref[...] = src_ref[...].astype(dst_ref.dtype)
        main = refs[:n_main] + refs[first_out:first_out + len(outs)] + refs[first_out + len(outs) + n_cast:]
        _mm_kernel(*main, n_a=len(a_list), n_b=len(b_list), pairs=tuple(pairs), n_extra=len(extras),
                   n_out=len(outs), nk=nk, k_rem=k_rem, n_split=n_split, epilogue=epilogue)

    return pl.pallas_call(
        body,
        grid=(nj, nmb, nk),
        in_specs=in_specs,
        out_specs=out_specs,
        out_shape=out_shape,
        scratch_shapes=scratch,
        input_output_aliases=aliases,
        compiler_params=_params(3),
        name=name,
    )(*args)


def _store_head(o_ref, h, y):
    if len(o_ref.shape) == 3:
        rows = o_ref.shape[0]
        o_ref.reshape(rows * SUBLANES, HEAD_DIM)[pl.ds(h, rows, stride=SUBLANES), :] = y.astype(o_ref.dtype)
    else:
        o_ref[:, _head(h)] = y.astype(o_ref.dtype)


def _epi_store(accs, extra, outs):
    for o in outs:
        if len(o.shape) == 3:
            for h in range(accs[0].shape[1] // HEAD_DIM):
                _store_head(o, h, accs[0][:, _head(h)])
        else:
            o[...] = accs[0].astype(o.dtype)


def _epi_headnorm(accs, extra, outs):
    x = accs[0]
    g = extra[0][...]
    for h in range(x.shape[1] // HEAD_DIM):
        xh = x[:, _head(h)]
        ms = jnp.mean(xh * xh, axis=-1, keepdims=True)
        y = (xh * lax.rsqrt(ms + EPS)) * g[:, _head(h)]
        for o in outs:
            _store_head(o, h, y)


def _epi_merge(accs, extra, outs):
    ga = jax.nn.sigmoid(extra[0][...])
    gb = jax.nn.sigmoid(extra[1][...])
    outs[0][...] = (ga * accs[0] + gb * accs[1]).astype(outs[0].dtype)


def _epi_conv(accs, extra, outs, *, scratch, seq_blocks):
    bg, cg, xc = accs
    w = extra[0][...]
    buf_ref, carry_ref = scratch
    rows = bg.shape[0]
    u = cg * xc
    starts_sequence = pl.program_id(1) % seq_blocks == 0
    buf_ref[SUBLANES - 2:SUBLANES, :] = jnp.where(starts_sequence, extra[1][...], carry_ref[...])
    buf_ref[SUBLANES:, :] = u
    u1 = buf_ref[pl.ds(SUBLANES - 1, rows), :]
    u2 = buf_ref[pl.ds(SUBLANES - 2, rows), :]
    y = u2 * w[0:1, :] + u1 * w[1:2, :] + u * w[2:3, :]
    outs[0][...] = (bg * y).astype(outs[0].dtype)
    last2 = buf_ref[pl.ds(rows + SUBLANES - 2, 2), :]
    carry_ref[...] = last2
    outs[1][...] = last2


def _epi_resid(accs, extra, outs):
    outs[0][...] = extra[0][...] + accs[0]


def _epi_swiglu(accs, extra, outs):
    g = accs[0]
    outs[0][...] = ((g * jax.nn.sigmoid(g)) * accs[1]).astype(outs[0].dtype)


def _conv_kernel(bg_ref, cg_ref, xc_ref, cgp_ref, xcp_ref, init_ref, w_ref, ob_ref, cn_ref, buf_ref, *, ts):
    s = pl.program_id(2)
    u = cg_ref[...] * xc_ref[...]
    prev_u = cgp_ref[SUBLANES - 2:, :] * xcp_ref[SUBLANES - 2:, :]
    prev = jnp.where(s == 0, init_ref[...], prev_u)
    buf_ref[SUBLANES - 2:SUBLANES, :] = prev
    buf_ref[SUBLANES:, :] = u
    u1 = buf_ref[pl.ds(SUBLANES - 1, ts), :]
    u2 = buf_ref[pl.ds(SUBLANES - 2, ts), :]
    w = w_ref[...]
    y = u2 * w[0:1, :] + u1 * w[1:2, :] + u * w[2:3, :]
    ob_ref[...] = (bg_ref[...] * y).astype(ob_ref.dtype)
    cn_ref[...] = buf_ref[pl.ds(ts + SUBLANES - 2, 2), :]


def _short_conv(bcx, init, w, batch, seq):
    c = w.shape[1]
    ts = _pick(seq, 1024)
    tc = _pick(c, 1024)
    ncb = c // tc
    x3 = bcx.reshape(batch, seq, 3 * c)
    sub = ts // SUBLANES

    def cur(region):
        return pl.BlockSpec((None, ts, tc), lambda b, j, s, r=region: (b, s, j + r * ncb))

    def prev(region):
        return pl.BlockSpec((None, SUBLANES, tc),
                            lambda b, j, s, r=region: (b, jnp.maximum(s * sub - 1, 0), j + r * ncb))

    ob, cn = pl.pallas_call(
        functools.partial(_conv_kernel, ts=ts),
        grid=(batch, ncb, seq // ts),
        in_specs=[cur(0), cur(1), cur(2), prev(1), prev(2),
                  pl.BlockSpec((None, 2, tc), lambda b, j, s: (b, 0, j)),
                  pl.BlockSpec((3, tc), lambda b, j, s: (0, j))],
        out_specs=[pl.BlockSpec((None, ts, tc), lambda b, j, s: (b, s, j)),
                   pl.BlockSpec((None, 2, tc), lambda b, j, s: (b, 0, j))],
        out_shape=[jax.ShapeDtypeStruct((batch, seq, c), BF16),
                   jax.ShapeDtypeStruct((batch, 2, c), F32)],
        scratch_shapes=[pltpu.VMEM((ts + SUBLANES, tc), F32)],
        compiler_params=_params(3),
        name="short_conv",
    )(x3, x3, x3, x3, x3, init, w)
    return ob.reshape(batch * seq, c), cn


def _sb_heads(qs, kbs, vbs, u_tri, carries, accs, masks, scale):
    chains = [(h, b) for h in range(len(qs)) for b in range(len(kbs[h]))]
    dot = functools.partial(jnp.dot, preferred_element_type=F32)
    z = {c: lax.dot_general(qs[c[0]], kbs[c[0]][c[1]], (((1,), (1,)), ((), ())),
                            preferred_element_type=F32) * scale for c in chains}
    sp = {c: jnp.maximum(z[c], 0.0) + jnp.log(1.0 + jnp.exp(-jnp.abs(z[c]))) for c in chains}
    lneg = {c: -sp[c] if masks[c[1]] is None else jnp.where(masks[c[1]], -sp[c], 0.0) for c in chains}
    hi = {c: lneg[c].astype(BF16) for c in chains}
    lo = {c: (lneg[c] - hi[c].astype(F32)).astype(BF16) for c in chains}
    tot = {c: jnp.sum(lneg[c], axis=1, keepdims=True) for c in chains}
    tail = {c: dot(jnp.concatenate([hi[c], lo[c]], axis=1), u_tri) for c in chains}
    new_carries, new_accs = [], []
    for h in range(len(qs)):
        carry, acc = carries[h], accs[h]
        for b in range(len(kbs[h])):
            c = (h, b)
            a = jnp.exp((z[c] - sp[c]) + tail[c] + carry)
            if masks[b] is not None:
                a = jnp.where(masks[b], a, 0.0)
            acc = acc + dot(a.astype(BF16), vbs[h][b])
            carry = carry + tot[c]
        new_carries.append(carry)
        new_accs.append(acc)
    return tuple(new_carries), tuple(new_accs)


def _any_weight_left(carries):
    m = carries[0]
    for c in carries[1:]:
        m = jnp.maximum(m, c)
    return (jnp.max(m) > LOG_WEIGHT_FLOOR).astype(jnp.int32)


def _sba_prompt_kernel(q_ref, k_ref, v_ref, u_ref, o_ref, *, t, hp, scale):
    i = pl.program_id(2)
    u_tri = u_ref[...]
    diff = lax.broadcasted_iota(jnp.int32, (t, t), 1) - lax.broadcasted_iota(jnp.int32, (t, t), 0)
    qs = [q_ref[:, _head(h)] for h in range(hp)]

    def kv(j):
        st = pl.multiple_of(j * t, t)
        return ([k_ref[pl.ds(st, t), _head(h)] for h in range(hp)],
                [v_ref[pl.ds(st, t), _head(h)] for h in range(hp)])

    first = jnp.maximum(i - 1, 0)
    k_lo, v_lo = kv(first)
    k_hi, v_hi = kv(first + 1)
    masks = [diff < (i - first - 1) * t, diff < (i - first) * t]
    zero_c = tuple(jnp.zeros((t, 1), F32) for _ in range(hp))
    zero_a = tuple(jnp.zeros((t, HEAD_DIM), F32) for _ in range(hp))
    carries, accs = _sb_heads(qs, [[k_hi[h], k_lo[h]] for h in range(hp)], [[v_hi[h], v_lo[h]] for h in range(hp)],
                              u_tri, zero_c, zero_a, masks, scale)

    def cond(c):
        return jnp.logical_and(c[0] >= 0, c[1] > 0)

    def body(c):
        j, _, carries, accs = c
        kj, vj = kv(j)
        carries, accs = _sb_heads(qs, [[x] for x in kj], [[x] for x in vj], u_tri, carries, accs, [None], scale)
        return j - 1, _any_weight_left(carries), carries, accs

    _, _, carries, accs = lax.while_loop(cond, body, (first - 1, _any_weight_left(carries), carries, accs))
    for h in range(hp):
        o_ref[:, _head(h)] = accs[h].astype(o_ref.dtype)


def _tri(t):
    r = lax.broadcasted_iota(jnp.int32, (t, t), 0)
    c = lax.broadcasted_iota(jnp.int32, (t, t), 1)
    tri = (r > c).astype(BF16)
    return jnp.concatenate([tri, tri], axis=0)


def _sba_prompt(q, k, v, batch, seq):
    d_att = q.shape[1]
    heads = d_att // HEAD_DIM
    hp = 2 if heads % 2 == 0 else 1
    t = _pick(seq, 256)
    assert seq >= 2 * t
    q3, k3, v3 = (a.reshape(batch, seq, d_att) for a in (q, k, v))
    out = pl.pallas_call(
        functools.partial(_sba_prompt_kernel, t=t, hp=hp, scale=HEAD_DIM ** -0.5),
        grid=(batch, heads // hp, seq // t),
        in_specs=[pl.BlockSpec((None, t, hp * HEAD_DIM), lambda b, g, i: (b, i, g)),
                  pl.BlockSpec((None, seq, hp * HEAD_DIM), lambda b, g, i: (b, 0, g)),
                  pl.BlockSpec((None, seq, hp * HEAD_DIM), lambda b, g, i: (b, 0, g)),
                  pl.BlockSpec((2 * t, t), lambda b, g, i: (0, 0))],
        out_specs=pl.BlockSpec((None, t, hp * HEAD_DIM), lambda b, g, i: (b, i, g)),
        out_shape=jax.ShapeDtypeStruct((batch, seq, d_att), BF16),
        compiler_params=_params(3),
        name="sba_prompt",
    )(q3, k3, v3, _tri(t))
    return out.reshape(batch * seq, d_att)


def _sba_sample_kernel(q_ref, kn_ref, vn_ref, kc_ref, vc_ref, u_ref, o_ref, carry_ref, acc_ref, alive_ref,
                       *, nq, tk, heads, scale):
    s = pl.program_id(1)
    u_tri = u_ref[...]

    hs = range(heads)
    qs = [q_ref[:, _head(h)] for h in hs]

    def step(kbs, vbs, carries, accs, mask):
        carries, accs = _sb_heads(qs, [[x] for x in kbs], [[x] for x in vbs], u_tri, carries, accs, [mask], scale)
        for h in hs:
            carry_ref[h] = carries[h]
            acc_ref[h] = accs[h]
        alive_ref[0] = _any_weight_left(carries)

    @pl.when(s == 0)
    def _():
        row = lax.broadcasted_iota(jnp.int32, (nq, tk), 0)
        col = lax.broadcasted_iota(jnp.int32, (nq, tk), 1)
        step([kn_ref[:, _head(h)] for h in hs], [vn_ref[:, _head(h)] for h in hs],
             [jnp.zeros((nq, 1), F32) for _ in hs], [jnp.zeros((nq, HEAD_DIM), F32) for _ in hs], col < row)

    @pl.when(alive_ref[0] > 0)
    def _():
        step([kc_ref[pl.ds(h, tk, stride=heads), :].astype(BF16) for h in hs],
             [vc_ref[pl.ds(h, tk, stride=heads), :].astype(BF16) for h in hs],
             [carry_ref[h] for h in hs], [acc_ref[h] for h in hs], None)

    for h in range(heads):
        o_ref[:, _head(h)] = acc_ref[h].astype(o_ref.dtype)


def _sba_sample(q, k_new, v_new, cache_k, cache_v, layer, batch, nq):
    d_att = q.shape[1]
    heads = d_att // HEAD_DIM
    depth, _, past = cache_k.shape[:3]
    tk = _pick(past, 128)
    assert nq <= tk and past % tk == 0
    nkb = past // tk
    q3 = q.reshape(batch, nq, d_att)
    pad = lambda a: jnp.pad(a.reshape(batch, nq, d_att), ((0, 0), (0, tk - nq), (0, 0)))
    kc = cache_k.reshape(depth, batch, past * heads, HEAD_DIM)
    vc = cache_v.reshape(depth, batch, past * heads, HEAD_DIM)
    cache_spec = pl.BlockSpec((None, None, tk * heads, HEAD_DIM), lambda b, s: (layer, b, nkb - 1 - s, 0))
    out = pl.pallas_call(
        functools.partial(_sba_sample_kernel, nq=nq, tk=tk, heads=heads, scale=HEAD_DIM ** -0.5),
        grid=(batch, nkb),
        in_specs=[pl.BlockSpec((None, nq, d_att), lambda b, s: (b, 0, 0)),
                  pl.BlockSpec((None, tk, d_att), lambda b, s: (b, 0, 0)),
                  pl.BlockSpec((None, tk, d_att), lambda b, s: (b, 0, 0)),
                  cache_spec, cache_spec,
                  pl.BlockSpec((2 * tk, tk), lambda b, s: (0, 0))],
        out_specs=pl.BlockSpec((None, nq, d_att), lambda b, s: (b, 0, 0)),
        out_shape=jax.ShapeDtypeStruct((batch, nq, d_att), BF16),
        scratch_shapes=[pltpu.VMEM((heads, nq, 1), F32), pltpu.VMEM((heads, nq, HEAD_DIM), F32),
                        pltpu.SMEM((1,), jnp.int32)],
        compiler_params=_params(2),
        name="sba_sample",
    )(q3, pad(k_new), pad(v_new), kc, vc, _tri(tk))
    return out.reshape(batch * nq, d_att)


def _layer(x, batch, seq, conv_init, cache, w, src, layer, depth, kv_bufs, bm):
    d_model = x.shape[1]
    d_att = w["g_qt"].shape[1]
    d_conv = w["w_conv"].shape[1]
    w_in = w["w_in"]
    bn = SUBLANES * HEAD_DIM

    def mm(*args, makes=(), **kw):
        cast = [(src[name], layer) for name in makes] if src is not None else []
        res = _matmul(*args, cast=cast, **kw)
        for name, arr in zip(makes if src is not None else (), res[len(res) - len(cast):]):
            w[name] = arr
        return res[:len(res) - len(cast)]

    h = _rmsnorm(x, w["g_mix"][layer])
    (q,) = mm([h], [(w_in, None, 0)], [(0, 0)], [(w["g_qt"], "row", 0)], [BF16], _epi_headnorm,
              n_cols=d_att, bm=bm, bn=bn, name="proj_q", makes=("w_br",))
    k_buf, k16 = mm([h], [(w_in, None, d_att)], [(0, 0)], [(w["g_kt"], "row", 0)],
                    [("heads", F32, kv_bufs[0], depth, layer), BF16], _epi_headnorm,
                    n_cols=d_att, bm=bm, bn=bn, name="proj_k")
    v_buf, v16 = mm([h], [(w_in, None, 2 * d_att)], [(0, 0)], [],
                    [("heads", F32, kv_bufs[1], depth, layer), BF16], _epi_store,
                    n_cols=d_att, bm=bm, bn=bn, name="proj_v")
    rows = _pick(x.shape[0], bm)
    fuse_conv = seq % rows == 0
    if fuse_conv:
        o_b, conv_new = mm([h], [(w_in, None, 3 * d_att + r * d_conv) for r in range(3)], [(0, 0), (0, 1), (0, 2)],
                           [(w["w_conv"], "row", 0), (conv_init, "state", 0)], [BF16, ("state", F32, 2)],
                           functools.partial(_epi_conv, seq_blocks=seq // rows),
                           n_cols=d_conv, bm=bm, bn=256, seq_blocks=seq // rows,
                           scratch_rows=[(SUBLANES, True), (2, False)], name="proj_conv", makes=("w_out",))
    else:
        (bcx,) = mm([h], [(w_in, None, 3 * d_att)], [(0, 0)], [], [F32], _epi_store,
                    n_cols=3 * d_conv, bm=bm, bn=bn, name="proj_conv", makes=("w_out",))
    (gl,) = mm([h], [(w_in, None, 3 * d_att + 3 * d_conv)], [(0, 0)], [], [F32], _epi_store,
               n_cols=2 * d_model, bm=bm, bn=bn, name="proj_gate", makes=("w1",))
    if cache is None:
        o_a = _sba_prompt(q, k16, v16, batch, seq)
    else:
        o_a = _sba_sample(q, k16, v16, cache[0], cache[1], layer, batch, seq)
    if not fuse_conv:
        o_b, conv_new = _short_conv(bcx, conv_init, w["w_conv"], batch, seq)
    w_br = w["w_br"].reshape(2, -1, d_model)
    (m,) = mm([o_a, o_b], [(w_br, 0, 0), (w_br, 1, 0)], [(0, 0), (1, 1)],
              [(gl, "tile", 0), (gl, "tile", d_model)], [BF16], _epi_merge,
              n_cols=d_model, bm=bm, bn=512, name="merge", makes=("w3",))
    (x,) = mm([m], [(w["w_out"], None, 0)], [(0, 0)], [(x, "tile", 0)], [F32], _epi_resid,
              n_cols=d_model, bm=bm, bn=bn, name="out_proj")
    h2 = _rmsnorm(x, w["g_ffn"][layer])
    d_ff = w["w1"].shape[1]
    (hid,) = mm([h2], [(w["w1"], None, 0), (w["w3"], None, 0)], [(0, 0), (0, 1)], [], [BF16], _epi_swiglu,
                n_cols=d_ff, bm=bm, bn=512, n_split=2, name="ffn_up", makes=("w2",))
    next_w_in = None
    if src is not None and layer + 1 < depth:
        x, next_w_in = _matmul([hid], [(w["w2"], None, 0)], [(0, 0)], [(x, "tile", 0)], [F32], _epi_resid,
                               n_cols=d_model, bm=bm, bn=bn, tk=min(d_ff, FF_TK), name="ffn_down",
                               cast=[(src["w_in"], layer + 1)])
    else:
        (x,) = _matmul([hid], [(w["w2"], None, 0)], [(0, 0)], [(x, "tile", 0)], [F32], _epi_resid,
                       n_cols=d_model, bm=bm, bn=bn, tk=min(d_ff, FF_TK), name="ffn_down")
    return x, k_buf, v_buf, conv_new, next_w_in


def kernel(x_prompt, x_sample, cache_k, cache_v, state_conv, g_mix, w_in, g_q, g_k, w_conv, w_br, w_out,
           g_ffn, w_ffn_gate, w_ffn_up, w_ffn_down):
    batch, seq, d_model = x_prompt.shape
    dec_batch, dec_seq, _ = x_sample.shape
    depth = w_in.shape[0]
    d_conv = w_conv.shape[2]
    heads = cache_k.shape[3]
    d_att = heads * HEAD_DIM

    src = {"w_in": w_in, "w_br": w_br.reshape(depth, -1, d_model), "w_out": w_out,
           "w1": w_ffn_gate, "w3": w_ffn_up, "w2": w_ffn_down}
    xp = x_prompt.reshape(batch * seq, d_model)
    xs = x_sample.reshape(dec_batch * dec_seq, d_model)
    zero_state = jnp.zeros((batch, 2, d_conv), F32)
    kv_zeros = lambda rows: jnp.zeros((depth * rows, heads // SUBLANES, SUBLANES, HEAD_DIM), F32)
    kvp = (kv_zeros(batch * seq), kv_zeros(batch * seq))
    kvs = (kv_zeros(dec_batch * dec_seq), kv_zeros(dec_batch * dec_seq))
    cp, cs = [], []
    w_in_l = w_in[0].astype(BF16)
    for l in range(depth):
        w = {"g_mix": g_mix, "g_ffn": g_ffn, "w_in": w_in_l, "w_conv": w_conv[l],
             "g_qt": jnp.tile(g_q[l], heads).reshape(1, d_att),
             "g_kt": jnp.tile(g_k[l], heads).reshape(1, d_att)}
        xp, kb, vb, cn, w_in_l = _layer(xp, batch, seq, zero_state, None, w, src, l, depth, kvp, bm=1024)
        kvp = (kb, vb)
        cp.append(cn)
        xs, kb, vb, cn, _ = _layer(xs, dec_batch, dec_seq, state_conv[l], (cache_k, cache_v), w, None, l, depth,
                                   kvs, bm=128)
        kvs = (kb, vb)
        cs.append(cn)

    kv_p = lambda buf: buf.reshape(depth, batch, seq, heads, HEAD_DIM)
    kv_s = lambda buf: buf.reshape(depth, dec_batch, dec_seq, heads, HEAD_DIM)
    return (xp.reshape(batch, seq, d_model), xs.reshape(dec_batch, dec_seq, d_model),
            kv_p(kvp[0]), kv_p(kvp[1]), jnp.stack(cp), kv_s(kvs[0]), kv_s(kvs[1]), jnp.stack(cs))
```
